```python
import jax
import jax.numpy as jnp
from jax import lax
import numpy as np

D_MODEL = 1024
BATCH = 16
SEQ = 4096
DEPTH = 2

CHUNK = 64
Q_BLOCK = 128
NORM_EPS = 1e-6

MLA_HEADS = 8
MLA_Q_LORA = 512
MLA_KV_LORA = 256
MLA_NOPE = 128
MLA_ROPE = 64
MLA_V = 128
MLA_WIDTH = MLA_HEADS * MLA_V
ROPE_BASE = 10000.0

RWKV_HEAD = 64
RWKV_HEADS = D_MODEL // RWKV_HEAD
RWKV_WIDTH = RWKV_HEADS * RWKV_HEAD
RWKV_DECAY_LORA = 64
RWKV_AAA_LORA = 64
RWKV_SHIFT_COLS = 3 * RWKV_WIDTH + RWKV_DECAY_LORA + RWKV_AAA_LORA
RWKV_GN_EPS = 64e-5

GLA_HEADS = 4
GLA_KEY_WIDTH = D_MODEL // 2
GLA_WIDTH = D_MODEL
GLA_DK = GLA_KEY_WIDTH // GLA_HEADS
GLA_DV = GLA_WIDTH // GLA_HEADS
GLA_GATE_LORA = 16
GLA_GATE_NORM = 16.0

IN_SIZES = (
    MLA_Q_LORA, MLA_KV_LORA, MLA_ROPE, MLA_WIDTH,
    RWKV_SHIFT_COLS, RWKV_WIDTH,
    GLA_KEY_WIDTH, GLA_KEY_WIDTH, GLA_WIDTH, GLA_GATE_LORA, GLA_WIDTH,
    D_MODEL, D_MODEL, D_MODEL,
)
N_IN = sum(IN_SIZES)

kernel_name = "hybrid_mla_rwkv7_gla_stream_block"


def _split(p, sizes):
    outs = []
    start = 0
    for size in sizes:
        outs.append(p[..., start:start + size])
        start += size
    return outs


def _rms_norm(x, gain, eps=NORM_EPS):
    xf = x.astype(jnp.float32)
    y = xf * lax.rsqrt(jnp.mean(xf * xf, axis=-1, keepdims=True) + eps)
    return (y * gain.astype(jnp.float32)).astype(x.dtype)


def _rope_tables(positions):
    inv_freq = ROPE_BASE ** (-jnp.arange(0, MLA_ROPE, 2, dtype=jnp.float32) / MLA_ROPE)
    ang = positions.astype(jnp.float32)[..., None] * inv_freq
    ang = jnp.concatenate([ang, ang], axis=-1)
    return jnp.cos(ang), jnp.sin(ang)


def _apply_rope(x, cos, sin):
    xf = x.astype(jnp.float32)
    half = xf.shape[-1] // 2
    rot = jnp.concatenate([-xf[..., half:], xf[..., :half]], axis=-1)
    return (xf * cos + rot * sin).astype(x.dtype)


def _chunk_causal_attention(q_nope, q_rope, k_nope, k_rope, v):
    b, s, h, _ = q_nope.shape
    n_blocks = s // Q_BLOCK
    scale = (MLA_NOPE + MLA_ROPE) ** -0.5
    key_chunk = jnp.arange(s) // CHUNK

    def block(args):
        qn, qr, blk = args
        scores = (jnp.einsum('bqhd,bkhd->bhqk', qn, k_nope, preferred_element_type=jnp.float32)
                  + jnp.einsum('bqhr,bkr->bhqk', qr, k_rope, preferred_element_type=jnp.float32))
        q_chunk = (blk * Q_BLOCK + jnp.arange(Q_BLOCK)) // CHUNK
        allowed = key_chunk[None, :] <= q_chunk[:, None]
        scores = jnp.where(allowed, scores * scale, -jnp.inf)
        probs = jax.nn.softmax(scores, axis=-1).astype(v.dtype)
        return jnp.einsum('bhqk,bkhv->bqhv', probs, v)

    qn_b = q_nope.reshape(b, n_blocks, Q_BLOCK, h, MLA_NOPE).transpose(1, 0, 2, 3, 4)
    qr_b = q_rope.reshape(b, n_blocks, Q_BLOCK, h, MLA_ROPE).transpose(1, 0, 2, 3, 4)
    out = lax.map(block, (qn_b, qr_b, jnp.arange(n_blocks)))
    return out.transpose(1, 0, 2, 3, 4).reshape(b, s, h, MLA_V)


def _mla_branch(c_q, c_kv, k_rope, z, cos, sin, q_norm, kv_norm, w_uq, w_ukv, w_o):
    b, s, _ = c_q.shape
    c_q = _rms_norm(c_q, q_norm)
    c_kv = _rms_norm(c_kv, kv_norm)
    q = (c_q @ w_uq).reshape(b, s, MLA_HEADS, MLA_NOPE + MLA_ROPE)
    kv = (c_kv @ w_ukv).reshape(b, s, MLA_HEADS, MLA_NOPE + MLA_V)
    q_nope, q_rope = q[..., :MLA_NOPE], q[..., MLA_NOPE:]
    k_nope, v = kv[..., :MLA_NOPE], kv[..., MLA_NOPE:]
    q_rope = _apply_rope(q_rope, cos[:, :, None, :], sin[:, :, None, :])
    k_rope = _apply_rope(k_rope, cos, sin)
    o = _chunk_causal_attention(q_nope, q_rope, k_nope, k_rope, v).reshape(b, s, MLA_WIDTH)
    return (o * jax.nn.silu(z)) @ w_o


def _rwkv7_branch(streams, z, mu, w0, w2, a0, a2, k_k, k_a, r_k, ln_w, ln_b, w_o):
    b, s, _ = streams.shape
    f32 = jnp.float32
    prev = jnp.pad(streams, ((0, 0), (1, 0), (0, 0)))[:, :-1]
    streams = streams + mu * (prev - streams)
    r, k, v, w_l, a_l = _split(streams, (RWKV_WIDTH, RWKV_WIDTH, RWKV_WIDTH,
                                         RWKV_DECAY_LORA, RWKV_AAA_LORA))
    w = -jax.nn.softplus(-(w0 + jnp.tanh(w_l) @ w2)) - 0.5
    decay = jnp.exp(-jnp.exp(w.astype(f32)))
    a = jax.nn.sigmoid(a0 + a_l @ a2)
    heads = lambda t: t.astype(f32).reshape(b, s, RWKV_HEADS, RWKV_HEAD)
    kk = heads(k * k_k)
    kk = kk / jnp.maximum(jnp.sqrt(jnp.sum(kk * kk, axis=-1, keepdims=True)), 1e-12)
    k = k * (1 + (a - 1) * k_a)
    r_h, k_h, v_h, w_h, a_h = heads(r), heads(k), heads(v), heads(decay), heads(a)

    def step(state, inp):
        r_t, w_t, k_t, v_t, kk_t, a_t = inp
        sa = jnp.einsum('bhvk,bhk->bhv', state, -kk_t)
        state = (state * w_t[:, :, None, :]
                 + sa[..., None] * (kk_t * a_t)[:, :, None, :]
                 + v_t[..., None] * k_t[:, :, None, :])
        return state, jnp.einsum('bhvk,bhk->bhv', state, r_t)

    to_steps = lambda t: t.transpose(1, 0, 2, 3)
    state0 = jnp.zeros((b, RWKV_HEADS, RWKV_HEAD, RWKV_HEAD), f32)
    _, y = lax.scan(step, state0, tuple(to_steps(t) for t in (r_h, w_h, k_h, v_h, kk, a_h)))
    y = y.transpose(1, 0, 2, 3)
    mean = jnp.mean(y, axis=-1, keepdims=True)
    var = jnp.mean(jnp.square(y - mean), axis=-1, keepdims=True)
    y = ((y - mean) * lax.rsqrt(var + RWKV_GN_EPS)).reshape(b, s, RWKV_WIDTH)
    y = y * ln_w.astype(f32) + ln_b.astype(f32)
    bonus = jnp.sum(r_h * k_h * r_k.astype(f32), axis=-1, keepdims=True) * v_h
    y = (y + bonus.reshape(b, s, RWKV_WIDTH)).astype(z.dtype)
    return (y * jax.nn.silu(z)) @ w_o


def _gla_branch(q, k, v, g_l, z, w2, g_bias, norm_g, w_o):
    b, s, _ = q.shape
    nc = s // CHUNK
    f32 = jnp.float32
    q = q.astype(f32).reshape(b, nc, CHUNK, GLA_HEADS, GLA_DK) * (GLA_DK ** -0.5)
    k = k.astype(f32).reshape(b, nc, CHUNK, GLA_HEADS, GLA_DK)
    v = v.astype(f32).reshape(b, nc, CHUNK, GLA_HEADS, GLA_DV)
    log_a = jax.nn.log_sigmoid((g_l @ w2 + g_bias).astype(f32)) / GLA_GATE_NORM
    cum = jnp.cumsum(log_a.reshape(b, nc, CHUNK, GLA_HEADS, GLA_DK), axis=2)
    cum_last = cum[:, :, -1:]
    q_dec = q * jnp.exp(cum)
    k_inv = k * jnp.exp(-cum)
    k_to_end = k * jnp.exp(cum_last - cum)
    attn = jnp.einsum('bnihd,bnjhd->bnhij', q_dec, k_inv)
    attn = jnp.where(jnp.tril(jnp.ones((CHUNK, CHUNK), bool)), attn, 0.0)
    o_intra = jnp.einsum('bnhij,bnjhv->bnihv', attn, v)

    def step(state, inp):
        q_c, k_c, v_c, d_c = inp
        o_c = jnp.einsum('bihd,bhdv->bihv', q_c, state)
        state = state * d_c[..., None] + jnp.einsum('bjhd,bjhv->bhdv', k_c, v_c)
        return state, o_c

    mv = lambda t: jnp.moveaxis(t, 1, 0)
    state0 = jnp.zeros((b, GLA_HEADS, GLA_DK, GLA_DV), f32)
    _, o_inter = lax.scan(step, state0, (mv(q_dec), mv(k_to_end), mv(v), mv(jnp.exp(cum_last[:, :, 0]))))
    o = o_intra + mv(o_inter)
    o = o * lax.rsqrt(jnp.mean(o * o, axis=-1, keepdims=True) + NORM_EPS) * norm_g.astype(f32)
    o = o.reshape(b, s, GLA_WIDTH).astype(z.dtype)
    return (o * jax.nn.silu(z)) @ w_o


def setup_inputs(seed: int = 0) -> dict:
    key = jax.random.key(seed)
    ks = jax.random.split(key, 32)
    L = DEPTH
    f32 = jnp.float32
    nrm = lambda k, shape, sc: jax.random.normal(k, shape, f32) * sc
    gain = lambda k, shape: 1.0 + nrm(k, shape, 0.05)
    positions = (jax.random.randint(ks[2], (BATCH, 1), 0, 4096, dtype=jnp.int32)
                 + jnp.arange(SEQ, dtype=jnp.int32)[None, :])
    return {
        "x": nrm(ks[0], (BATCH, SEQ, D_MODEL), 1.0),
        "c": nrm(ks[1], (BATCH, D_MODEL), 1.0),
        "positions": positions,
        "ada_w": nrm(ks[3], (L, D_MODEL, 3 * D_MODEL), 0.5 * D_MODEL ** -0.5),
        "ada_b": nrm(ks[4], (L, 3 * D_MODEL), 0.02),
        "norm_pre": gain(ks[5], (L, D_MODEL)),
        "norm_post": gain(ks[6], (L, D_MODEL)),
        "w_in": nrm(ks[7], (L, D_MODEL, N_IN), D_MODEL ** -0.5),
        "rwkv_mu": jax.random.uniform(ks[8], (L, RWKV_SHIFT_COLS), f32),
        "mla_q_norm": gain(ks[9], (L, MLA_Q_LORA)),
        "mla_kv_norm": gain(ks[10], (L, MLA_KV_LORA)),
        "mla_w_uq": nrm(ks[11], (L, MLA_Q_LORA, MLA_HEADS * (MLA_NOPE + MLA_ROPE)), MLA_Q_LORA ** -0.5),
        "mla_w_ukv": nrm(ks[12], (L, MLA_KV_LORA, MLA_HEADS * (MLA_NOPE + MLA_V)), MLA_KV_LORA ** -0.5),
        "mla_w_o": nrm(ks[13], (L, MLA_WIDTH, D_MODEL), MLA_WIDTH ** -0.5),
        "rwkv_w0": jax.random.uniform(ks[14], (L, RWKV_WIDTH), f32, -6.0, -1.0),
        "rwkv_w2": nrm(ks[15], (L, RWKV_DECAY_LORA, RWKV_WIDTH), 0.1 * RWKV_DECAY_LORA ** -0.5),
        "rwkv_a0": nrm(ks[16], (L, RWKV_WIDTH), 0.1),
        "rwkv_a2": nrm(ks[17], (L, RWKV_AAA_LORA, RWKV_WIDTH), 0.1 * RWKV_AAA_LORA ** -0.5),
        "rwkv_k_k": 0.85 + nrm(ks[18], (L, RWKV_WIDTH), 0.05),
        "rwkv_k_a": gain(ks[19], (L, RWKV_WIDTH)),
        "rwkv_r_k": nrm(ks[20], (L, RWKV_HEADS, RWKV_HEAD), 0.1),
        "rwkv_ln_w": gain(ks[21], (L, RWKV_WIDTH)),
        "rwkv_ln_b": nrm(ks[22], (L, RWKV_WIDTH), 0.02),
        "rwkv_w_o": nrm(ks[23], (L, RWKV_WIDTH, D_MODEL), RWKV_WIDTH ** -0.5),
        "gla_w2": nrm(ks[24], (L, GLA_GATE_LORA, GLA_KEY_WIDTH), GLA_GATE_LORA ** -0.5),
        "gla_b": nrm(ks[25], (L, GLA_KEY_WIDTH), 0.1),
        "gla_norm": gain(ks[26], (L, GLA_DV)),
        "gla_w_o": nrm(ks[27], (L, GLA_WIDTH, D_MODEL), GLA_WIDTH ** -0.5),
        "w_out": nrm(ks[28], (L, D_MODEL, D_MODEL), D_MODEL ** -0.5),
    }


def reference(x, c, positions, ada_w, ada_b, norm_pre, norm_post, w_in, rwkv_mu,
              mla_q_norm, mla_kv_norm, mla_w_uq, mla_w_ukv, mla_w_o,
              rwkv_w0, rwkv_w2, rwkv_a0, rwkv_a2, rwkv_k_k, rwkv_k_a, rwkv_r_k,
              rwkv_ln_w, rwkv_ln_b, rwkv_w_o,
              gla_w2, gla_b, gla_norm, gla_w_o, w_out):
    cos, sin = _rope_tables(positions)
    c_act = jax.nn.silu(c)
    for l in range(DEPTH):
        mod = c_act @ ada_w[l] + ada_b[l]
        shift, scale, gate = jnp.split(mod, 3, axis=-1)
        h = _rms_norm(x, norm_pre[l]) * (1 + scale[:, None, :]) + shift[:, None, :]
        p = h @ w_in[l]
        (m_cq, m_ckv, m_kr, m_z, r_streams, r_z,
         g_q, g_k, g_v, g_l, g_z, gate_a, gate_b, gate_c) = _split(p, IN_SIZES)
        o_mla = _mla_branch(m_cq, m_ckv, m_kr, m_z, cos, sin, mla_q_norm[l], mla_kv_norm[l],
                            mla_w_uq[l], mla_w_ukv[l], mla_w_o[l])
        o_rwkv = _rwkv7_branch(r_streams, r_z, rwkv_mu[l], rwkv_w0[l], rwkv_w2[l], rwkv_a0[l],
                               rwkv_a2[l], rwkv_k_k[l], rwkv_k_a[l], rwkv_r_k[l],
                               rwkv_ln_w[l], rwkv_ln_b[l], rwkv_w_o[l])
        o_gla = _gla_branch(g_q, g_k, g_v, g_l, g_z, gla_w2[l], gla_b[l], gla_norm[l], gla_w_o[l])
        merged = (jax.nn.sigmoid(gate_a) * o_mla + jax.nn.sigmoid(gate_b) * o_rwkv
                  + jax.nn.sigmoid(gate_c) * o_gla)
        y = merged @ w_out[l]
        x = x + gate[:, None, :] * _rms_norm(y, norm_post[l])
    return x
```

```python
import functools

import jax
import jax.numpy as jnp
from jax import lax
from jax.experimental import pallas as pl
from jax.experimental.pallas import tpu as pltpu

f32 = jnp.float32
bf16 = jnp.bfloat16
HIGHEST = lax.Precision.HIGHEST

D = 1024
CHUNK = 64
NORM_EPS = 1e-6
MLA_HEADS = 8
MLA_NOPE = 128
MLA_ROPE = 64
MLA_V = 128
ROPE_BASE = 10000.0
RWKV_HEAD = 64
RWKV_GN_EPS = 64e-5
GLA_HEADS = 4
GLA_DK = 128
GLA_DV = 256
GLA_GATE_NORM = 16.0
LANES = 128
QUAD = 256
N_QUADS = D // QUAD
P_COLS = 12 * D
VMEM_LIMIT = 56 * 1024 * 1024

T_SMALL, T_MLA_Z, T_R, T_K, T_V, T_RZ, T_GQK, T_GV, T_GZ, T_GA, T_GB, T_GC = range(12)
OFF_CKV = 512
OFF_SLAB = 768
OFF_GL = 64
OFF_LORA = 896

NT_DIMS = (((1,), (1,)), ((), ()))
TN_DIMS = (((0,), (0,)), ((), ()))


def _div_pow2(x, n):
    return lax.shift_right_logical(x, jnp.int32(n.bit_length() - 1))


def _softplus(x):
    return jnp.maximum(x, 0.0) + jnp.log(1.0 + jnp.exp(-jnp.abs(x)))


def _sigmoid(x):
    return 1.0 / (1.0 + jnp.exp(-x))


def _silu(x):
    return x * _sigmoid(x)


def _rms(x, gain):
    return x * lax.rsqrt(jnp.mean(x * x, axis=-1, keepdims=True) + NORM_EPS) * gain


def _mod_kernel(c_ref, w_ref, b_ref, o_ref):
    c = _silu(c_ref[...]).astype(bf16)
    o_ref[0] = jnp.dot(c, w_ref[0].astype(bf16), preferred_element_type=f32) + b_ref[0]


def _modulation(c, ada_w, ada_b):
    depth, _, n3 = ada_w.shape
    batch = c.shape[0]
    return pl.pallas_call(
        _mod_kernel,
        grid=(depth, n3 // D),
        in_specs=[pl.BlockSpec((batch, D), lambda l, j: (0, 0)),
                  pl.BlockSpec((1, D, D), lambda l, j: (l, 0, j)),
                  pl.BlockSpec((1, 1, D), lambda l, j: (l, 0, j))],
        out_specs=pl.BlockSpec((1, batch, D), lambda l, j: (l, 0, j)),
        out_shape=jax.ShapeDtypeStruct((depth, batch, n3), f32),
        name="adaln_mod",
    )(c, ada_w, ada_b.reshape(depth, 1, n3))


def _inproj_kernel(x_ref, scale_ref, shift_ref, g_ref, w_ref, o_ref, h_ref):
    @pl.when(pl.program_id(1) == 0)
    def _():
        h = _rms(x_ref[...], g_ref[...]) * (1.0 + scale_ref[0]) + shift_ref[0]
        h_ref[...] = h.astype(bf16)

    o_ref[...] = jnp.dot(h_ref[...], w_ref[...], preferred_element_type=f32)


def _inproj(x2d, scale, shift, gain, w, seq):
    tokens = x2d.shape[0]
    tm = min(1024, seq)
    tn = D
    per_b = seq // tm
    return pl.pallas_call(
        _inproj_kernel,
        grid=(tokens // tm, P_COLS // tn),
        in_specs=[pl.BlockSpec((tm, D), lambda i, j: (i, 0)),
                  pl.BlockSpec((1, 1, D), lambda i, j: (i // per_b, 0, 0)),
                  pl.BlockSpec((1, 1, D), lambda i, j: (i // per_b, 0, 0)),
                  pl.BlockSpec((1, D), lambda i, j: (0, 0)),
                  pl.BlockSpec((D, tn), lambda i, j: (0, j))],
        out_specs=pl.BlockSpec((tm, tn), lambda i, j: (i, j)),
        out_shape=jax.ShapeDtypeStruct((tokens, P_COLS), f32),
        scratch_shapes=[pltpu.VMEM((tm, D), bf16)],
        compiler_params=pltpu.CompilerParams(
            dimension_semantics=("parallel", "arbitrary"), vmem_limit_bytes=VMEM_LIMIT),
        name="norm_inproj",
    )(x2d, scale, shift, gain, w)


def _mla_prep_kernel(p_ref, pos_ref, invf_ref, qn_ref, kvn_ref, wq_ref, wkv_ref,
                     q_ref, k_ref, v_ref):
    p = p_ref[...]
    tm = p.shape[0]
    cq = _rms(p[:, :OFF_CKV], qn_ref[...]).astype(bf16)
    ckv = _rms(p[:, OFF_CKV:OFF_SLAB], kvn_ref[...]).astype(bf16)
    slab = p[:, OFF_SLAB:OFF_SLAB + LANES]
    ang = pos_ref[...].astype(f32) * invf_ref[...]
    cos = jnp.cos(ang)
    sin = jnp.sin(ang)
    lane = lax.broadcasted_iota(jnp.int32, (tm, LANES), 1)
    half = MLA_ROPE // 2
    rot = jnp.where(lane < half, -pltpu.roll(slab, LANES - half, 1), pltpu.roll(slab, half, 1))
    k_rope = (slab * cos + rot * sin)[:, :MLA_ROPE].astype(bf16)
    cos_sin = jnp.where(lane < MLA_ROPE, cos, sin)
    scale = (MLA_NOPE + MLA_ROPE) ** -0.5
    for h in range(MLA_HEADS):
        rq = jnp.dot(cq, wq_ref[h], preferred_element_type=f32)
        a = rq[:, MLA_NOPE:] * cos_sin
        q_rope = a + pltpu.roll(a, MLA_ROPE, 1)
        q_ref[0, h, :, :MLA_NOPE] = (rq[:, :MLA_NOPE] * scale).astype(bf16)
        q_ref[0, h, :, MLA_NOPE:] = (q_rope[:, :MLA_ROPE] * scale).astype(bf16)
        rkv = jnp.dot(ckv, wkv_ref[h], preferred_element_type=f32)
        k_ref[0, h, :, :MLA_NOPE] = rkv[:, :MLA_NOPE].astype(bf16)
        k_ref[0, h, :, MLA_NOPE:] = k_rope
        v_ref[0, h] = rkv[:, MLA_NOPE:].astype(bf16)


def _mla_prep(p, pos2d, invf, q_norm, kv_norm, wq, wkv, batch, seq):
    tm = min(512, seq)
    per_b = seq // tm
    dqk = MLA_NOPE + MLA_ROPE
    const2 = lambda b, i: (0, 0)
    const3 = lambda b, i: (0, 0, 0)
    return pl.pallas_call(
        _mla_prep_kernel,
        grid=(batch, per_b),
        in_specs=[pl.BlockSpec((tm, D), lambda b, i: (b * per_b + i, T_SMALL)),
                  pl.BlockSpec((tm, 1), lambda b, i: (b * per_b + i, 0)),
                  pl.BlockSpec((1, LANES), const2),
                  pl.BlockSpec((1, OFF_CKV), const2),
                  pl.BlockSpec((1, OFF_SLAB - OFF_CKV), const2),
                  pl.BlockSpec(wq.shape, const3),
                  pl.BlockSpec(wkv.shape, const3)],
        out_specs=[pl.BlockSpec((1, MLA_HEADS, tm, dqk), lambda b, i: (b, 0, i, 0)),
                   pl.BlockSpec((1, MLA_HEADS, tm, dqk), lambda b, i: (b, 0, i, 0)),
                   pl.BlockSpec((1, MLA_HEADS, tm, MLA_V), lambda b, i: (b, 0, i, 0))],
        out_shape=[jax.ShapeDtypeStruct((batch, MLA_HEADS, seq, dqk), bf16),
                   jax.ShapeDtypeStruct((batch, MLA_HEADS, seq, dqk), bf16),
                   jax.ShapeDtypeStruct((batch, MLA_HEADS, seq, MLA_V), bf16)],
        compiler_params=pltpu.CompilerParams(
            dimension_semantics=("parallel", "parallel"), vmem_limit_bytes=VMEM_LIMIT),
        name="mla_prep",
    )(p, pos2d, invf, q_norm, kv_norm, wq, wkv)


def _flash_kernel(q_ref, k_ref, v_ref, z_ref, o_ref, m_ref, l_ref, acc_ref, *, tq, tk):
    qi = pl.program_id(2)
    q = q_ref[0, 0]
    m_ref[...] = jnp.full(m_ref.shape, -jnp.inf, f32)
    l_ref[...] = jnp.zeros(l_ref.shape, f32)
    acc_ref[...] = jnp.zeros(acc_ref.shape, f32)

    def step(ki, masked):
        start = pl.multiple_of(ki * tk, tk)
        k = k_ref[0, 0, pl.ds(start, tk), :]
        v = v_ref[0, 0, pl.ds(start, tk), :]
        s = lax.dot_general(q, k, NT_DIMS, preferred_element_type=f32)
        if masked:
            q_chunk = (_div_pow2(lax.broadcasted_iota(jnp.int32, (tq, tk), 0), CHUNK)
                       + qi * (tq // CHUNK))
            k_chunk = (_div_pow2(lax.broadcasted_iota(jnp.int32, (tq, tk), 1), CHUNK)
                       + ki * (tk // CHUNK))
            s = jnp.where(k_chunk <= q_chunk, s, -jnp.inf)
        m_prev = m_ref[...]
        m_new = jnp.maximum(m_prev, jnp.max(s, axis=-1, keepdims=True))
        alpha = jnp.exp(m_prev - m_new)
        prob = jnp.exp(s - m_new)
        l_ref[...] = alpha * l_ref[...] + jnp.sum(prob, axis=-1, keepdims=True)
        acc_ref[...] = alpha * acc_ref[...] + jnp.dot(prob.astype(bf16), v,
                                                      preferred_element_type=f32)
        m_ref[...] = m_new

    per_q = tq // tk
    n_full = qi * per_q

    def body(ki, carry):
        step(ki, False)
        return carry

    lax.fori_loop(0, n_full, body, 0)
    for j in range(per_q):
        step(n_full + j, True)
    o = acc_ref[...] / l_ref[...]
    o_ref[0] = (o * _silu(z_ref[...])).astype(bf16)


def _flash(q, k, v, p, batch, seq):
    tq = min(512, seq)
    tk = tq
    nq = seq // tq
    dqk = MLA_NOPE + MLA_ROPE
    z_col0 = T_MLA_Z * (D // MLA_V)
    return pl.pallas_call(
        functools.partial(_flash_kernel, tq=tq, tk=tk),
        grid=(batch, MLA_HEADS, nq),
        in_specs=[pl.BlockSpec((1, 1, tq, dqk), lambda b, h, i: (b, h, i, 0)),
                  pl.BlockSpec((1, 1, seq, dqk), lambda b, h, i: (b, h, 0, 0)),
                  pl.BlockSpec((1, 1, seq, MLA_V), lambda b, h, i: (b, h, 0, 0)),
                  pl.BlockSpec((tq, MLA_V), lambda b, h, i: (b * nq + i, z_col0 + h))],
        out_specs=pl.BlockSpec((1, tq, MLA_V), lambda b, h, i: (b, i, h)),
        out_shape=jax.ShapeDtypeStruct((batch, seq, MLA_HEADS * MLA_V), bf16),
        scratch_shapes=[pltpu.VMEM((tq, 1), f32), pltpu.VMEM((tq, 1), f32),
                        pltpu.VMEM((tq, MLA_V), f32)],
        compiler_params=pltpu.CompilerParams(
            dimension_semantics=("parallel", "parallel", "arbitrary"),
            vmem_limit_bytes=VMEM_LIMIT),
        name="mla_flash",
    )(q, k, v, p)


def _shift_mix(s, prev_row, mu):
    first = lax.broadcasted_iota(jnp.int32, s.shape, 0) == 0
    prev = jnp.where(first, prev_row, pltpu.roll(s, 1, 0))
    return s + mu * (prev - s)


def _rwkv_kernel(r_ref, k_ref, v_ref, lora_ref, z_ref, mu_ref, mul_ref, w0_ref, a0_ref, wl_ref,
                 kk_ref, ka_ref, rk_ref, lnw_ref, lnb_ref, seg_ref, ltri_ref,
                 o_ref, state_ref, prev_ref, prevl_ref):
    c_len = r_ref.shape[0]

    @pl.when(pl.program_id(1) == 0)
    def _():
        state_ref[...] = jnp.zeros(state_ref.shape, f32)
        prev_ref[...] = jnp.zeros(prev_ref.shape, f32)
        prevl_ref[...] = jnp.zeros(prevl_ref.shape, f32)

    r_raw, k_raw, v_raw, lora_raw = r_ref[...], k_ref[...], v_ref[...], lora_ref[...]
    r = _shift_mix(r_raw, prev_ref[0:1, :], mu_ref[0:1, :])
    k = _shift_mix(k_raw, prev_ref[1:2, :], mu_ref[1:2, :])
    v = _shift_mix(v_raw, prev_ref[2:3, :], mu_ref[2:3, :])
    lora = _shift_mix(lora_raw, prevl_ref[0:1, :], mul_ref[...])
    prev_ref[0:1, :] = r_raw[c_len - 1:c_len, :]
    prev_ref[1:2, :] = k_raw[c_len - 1:c_len, :]
    prev_ref[2:3, :] = v_raw[c_len - 1:c_len, :]
    prevl_ref[0:1, :] = lora_raw[c_len - 1:c_len, :]

    lane = lax.broadcasted_iota(jnp.int32, lora.shape, 1)
    lora = jnp.where(lane < LANES // 2, jnp.tanh(lora), lora)
    proj = jnp.dot(lora.astype(bf16), wl_ref[...], preferred_element_type=f32)
    w = -_softplus(-(w0_ref[...] + proj[:, :D])) - 0.5
    log_decay = -jnp.exp(w)
    a = _sigmoid(a0_ref[...] + proj[:, D:])
    kk = k * kk_ref[...]
    k = k * (1.0 + (a - 1.0) * ka_ref[...])

    cum = jnp.dot(ltri_ref[...], log_decay, precision=HIGHEST, preferred_element_type=f32)
    cum_last = cum[c_len - 1:c_len, :]
    g_inc = jnp.exp(cum)
    g_inv = jnp.exp(-cum)
    g_exc = jnp.exp(cum - log_decay)
    g_end = jnp.exp(cum_last)

    seg = seg_ref[...]
    row = lax.broadcasted_iota(jnp.int32, (c_len, QUAD), 0)
    col = lax.broadcasted_iota(jnp.int32, (c_len, QUAD), 1)
    col_t = jnp.bitwise_and(col, c_len - 1)
    col_head = _div_pow2(col, RWKV_HEAD)
    strict = col_t < row
    incl = col_t <= row
    eye = (col_t == row).astype(f32)
    heads_per_quad = QUAD // RWKV_HEAD
    block_diag = (_div_pow2(lax.broadcasted_iota(jnp.int32, (QUAD, QUAD), 0), RWKV_HEAD)
                  == _div_pow2(lax.broadcasted_iota(jnp.int32, (QUAD, QUAD), 1), RWKV_HEAD))

    def stack(y):
        return jnp.concatenate(
            [jnp.where(col_head == g, y, 0.0) for g in range(heads_per_quad)], axis=0)

    def hdot(x, y, dims=None):
        if dims is None:
            return jnp.dot(x, y, precision=HIGHEST, preferred_element_type=f32)
        return lax.dot_general(x, y, dims, precision=HIGHEST, preferred_element_type=f32)

    def qprod(x, y):
        return hdot(x, stack(y))

    for qd in range(N_QUADS):
        sl = slice(qd * QUAD, (qd + 1) * QUAD)
        kk_q = kk[:, sl]
        norm = jnp.maximum(jnp.sqrt(hdot(kk_q * kk_q, seg)), 1e-12)
        kk_q = kk_q / norm
        beta = kk_q * a[:, sl]
        a_bar = -kk_q * g_exc[:, sl]
        b_til = beta * g_inv[:, sl]
        k_til = k[:, sl] * g_inv[:, sl]
        r_bar = r[:, sl] * g_inc[:, sl]
        b_end = b_til * g_end[:, sl]
        k_end = k_til * g_end[:, sl]
        v_q = v[:, sl]

        pair = hdot(jnp.concatenate([a_bar, r_bar], axis=0),
                    jnp.concatenate([stack(b_til), stack(k_til)], axis=0), NT_DIMS)
        a_ab = jnp.where(strict, pair[:c_len, :QUAD], 0.0)
        a_ak = jnp.where(strict, pair[:c_len, QUAD:], 0.0)
        a_rb = jnp.where(incl, pair[c_len:, :QUAD], 0.0)
        a_rk = jnp.where(incl, pair[c_len:, QUAD:], 0.0)

        t_inv = eye + a_ab
        power = a_ab
        span = 2
        while span < c_len:
            power = qprod(power, power)
            t_inv = t_inv + qprod(t_inv, power)
            span *= 2

        w_mat = qprod(t_inv, a_bar)
        u_loc = qprod(t_inv, qprod(a_ak, v_q))
        state = state_ref[qd]
        u = hdot(w_mat, state, NT_DIMS) + u_loc
        y = hdot(r_bar, state, NT_DIMS) + qprod(a_rb, u) + qprod(a_rk, v_q)
        upd = hdot(jnp.concatenate([u, v_q], axis=0),
                   jnp.concatenate([b_end, k_end], axis=0), TN_DIMS)
        state_ref[qd] = state * g_end[:, sl] + jnp.where(block_diag, upd, 0.0)

        inv_n = 1.0 / RWKV_HEAD
        mean = hdot(y, seg) * inv_n
        dev = y - mean
        var = hdot(dev * dev, seg) * inv_n
        y = dev * lax.rsqrt(var + RWKV_GN_EPS) * lnw_ref[:, sl] + lnb_ref[:, sl]
        bonus = hdot(r[:, sl] * k[:, sl] * rk_ref[:, sl], seg) * v_q
        o_ref[:, sl] = ((y + bonus) * _silu(z_ref[:, sl])).astype(bf16)


def _rwkv(p, mu_rkv, mu_lora, w0, a0, w_lora, k_k, k_a, r_k, ln_w, ln_b, seg, ltri, batch, seq):
    nc = seq // CHUNK
    tokens = batch * seq
    lora_col = T_SMALL * (D // LANES) + OFF_LORA // LANES
    tile = lambda t: pl.BlockSpec((CHUNK, D), lambda b, c: (b * nc + c, t))
    const = lambda shape: pl.BlockSpec(shape, lambda b, c: (0, 0))
    return pl.pallas_call(
        _rwkv_kernel,
        grid=(batch, nc),
        in_specs=[tile(T_R), tile(T_K), tile(T_V),
                  pl.BlockSpec((CHUNK, LANES), lambda b, c: (b * nc + c, lora_col)),
                  tile(T_RZ),
                  const(mu_rkv.shape), const(mu_lora.shape), const((1, D)), const((1, D)),
                  const(w_lora.shape),
                  const((1, D)), const((1, D)), const((1, D)), const((1, D)), const((1, D)),
                  const(seg.shape), const(ltri.shape)],
        out_specs=pl.BlockSpec((CHUNK, D), lambda b, c: (b * nc + c, 0)),
        out_shape=jax.ShapeDtypeStruct((tokens, D), bf16),
        scratch_shapes=[pltpu.VMEM((N_QUADS, QUAD, QUAD), f32),
                        pltpu.VMEM((8, D), f32), pltpu.VMEM((8, LANES), f32)],
        compiler_params=pltpu.CompilerParams(
            dimension_semantics=("parallel", "arbitrary"), vmem_limit_bytes=VMEM_LIMIT),
        name="rwkv7_chunk",
    )(p, p, p, p, p, mu_rkv, mu_lora, w0, a0, w_lora, k_k, k_a, r_k, ln_w, ln_b, seg, ltri)


def _gla_kernel(qk_ref, v_ref, gl_ref, z_ref, w2_ref, gb_ref, ng_ref, ltri_ref, o_ref, state_ref):
    c_len = qk_ref.shape[0]
    kw = GLA_HEADS * GLA_DK

    @pl.when(pl.program_id(1) == 0)
    def _():
        state_ref[...] = jnp.zeros(state_ref.shape, f32)

    qk = qk_ref[...]
    q = qk[:, :kw] * (GLA_DK ** -0.5)
    k = qk[:, kw:]
    gate = jnp.dot(gl_ref[...].astype(bf16), w2_ref[...], preferred_element_type=f32) + gb_ref[...]
    log_a = -_softplus(-gate) / GLA_GATE_NORM
    cum = jnp.dot(ltri_ref[...], log_a, precision=HIGHEST, preferred_element_type=f32)
    cum_last = cum[c_len - 1:c_len, :]
    q_dec = (q * jnp.exp(cum)).astype(bf16)
    k_inv = (k * jnp.exp(-cum)).astype(bf16)
    k_end = (k * jnp.exp(cum_last - cum)).astype(bf16)
    decay = jnp.exp(cum_last)
    causal = (lax.broadcasted_iota(jnp.int32, (c_len, c_len), 1)
              <= lax.broadcasted_iota(jnp.int32, (c_len, c_len), 0))
    for h in range(GLA_HEADS):
        ks = slice(h * GLA_DK, (h + 1) * GLA_DK)
        vs = slice(h * GLA_DV, (h + 1) * GLA_DV)
        v_h = v_ref[:, vs].astype(bf16)
        attn = lax.dot_general(q_dec[:, ks], k_inv[:, ks], NT_DIMS, preferred_element_type=f32)
        attn = jnp.where(causal, attn, 0.0).astype(bf16)
        state = state_ref[h]
        o = (jnp.dot(attn, v_h, preferred_element_type=f32)
             + lax.dot_general(q_dec[:, ks], state.astype(bf16), NT_DIMS,
                               preferred_element_type=f32))
        state_ref[h] = state * decay[:, ks] + lax.dot_general(
            v_h, k_end[:, ks], TN_DIMS, preferred_element_type=f32)
        o = o * lax.rsqrt(jnp.mean(o * o, axis=-1, keepdims=True) + NORM_EPS) * ng_ref[...]
        o_ref[:, vs] = (o * _silu(z_ref[:, vs])).astype(bf16)


def _gla(p, w2_pad, g_bias, norm_g, ltri, batch, seq):
    nc = seq // CHUNK
    tokens = batch * seq
    slab_col = T_SMALL * (D // LANES) + OFF_SLAB // LANES
    tile = lambda t: pl.BlockSpec((CHUNK, D), lambda b, c: (b * nc + c, t))
    const = lambda shape: pl.BlockSpec(shape, lambda b, c: (0, 0))
    return pl.pallas_call(
        _gla_kernel,
        grid=(batch, nc),
        in_specs=[tile(T_GQK), tile(T_GV),
                  pl.BlockSpec((CHUNK, LANES), lambda b, c: (b * nc + c, slab_col)),
                  tile(T_GZ),
                  const(w2_pad.shape), const(g_bias.shape), const(norm_g.shape),
                  const(ltri.shape)],
        out_specs=pl.BlockSpec((CHUNK, D), lambda b, c: (b * nc + c, 0)),
        out_shape=jax.ShapeDtypeStruct((tokens, D), bf16),
        scratch_shapes=[pltpu.VMEM((GLA_HEADS, GLA_DV, GLA_DK), f32)],
        compiler_params=pltpu.CompilerParams(
            dimension_semantics=("parallel", "arbitrary"), vmem_limit_bytes=VMEM_LIMIT),
        name="gla_chunk",
    )(p, p, p, p, w2_pad, g_bias, norm_g, ltri)


def _merge_kernel(om_ref, or_ref, og_ref, ga_ref, gb_ref, gc_ref, x_ref, gate_ref, np_ref,
                  wm_ref, wr_ref, wg_ref, wo_ref, o_ref):
    merged = (_sigmoid(ga_ref[...]) * jnp.dot(om_ref[...], wm_ref[...], preferred_element_type=f32)
              + _sigmoid(gb_ref[...]) * jnp.dot(or_ref[...], wr_ref[...],
                                                preferred_element_type=f32)
              + _sigmoid(gc_ref[...]) * jnp.dot(og_ref[...], wg_ref[...],
                                                preferred_element_type=f32))
    y = jnp.dot(merged.astype(bf16), wo_ref[...], preferred_element_type=f32)
    o_ref[...] = x_ref[...] + gate_ref[0] * _rms(y, np_ref[...])


def _merge(o_mla, o_rwkv, o_gla, p, x2d, gate, norm_post, w_mla, w_rwkv, w_gla, w_out, seq):
    tokens = x2d.shape[0]
    tm = min(512, seq)
    per_b = seq // tm
    row = lambda i: (i, 0)
    const = lambda i: (0, 0)
    return pl.pallas_call(
        _merge_kernel,
        grid=(tokens // tm,),
        in_specs=[pl.BlockSpec((tm, D), row), pl.BlockSpec((tm, D), row),
                  pl.BlockSpec((tm, D), row),
                  pl.BlockSpec((tm, D), lambda i: (i, T_GA)),
                  pl.BlockSpec((tm, D), lambda i: (i, T_GB)),
                  pl.BlockSpec((tm, D), lambda i: (i, T_GC)),
                  pl.BlockSpec((tm, D), row),
                  pl.BlockSpec((1, 1, D), lambda i: (i // per_b, 0, 0)),
                  pl.BlockSpec((1, D), const),
                  pl.BlockSpec((D, D), const), pl.BlockSpec((D, D), const),
                  pl.BlockSpec((D, D), const), pl.BlockSpec((D, D), const)],
        out_specs=pl.BlockSpec((tm, D), row),
        out_shape=jax.ShapeDtypeStruct((tokens, D), f32),
        compiler_params=pltpu.CompilerParams(
            dimension_semantics=("parallel",), vmem_limit_bytes=VMEM_LIMIT),
        name="merge_outproj",
    )(o_mla, o_rwkv, o_gla, p, p, p, x2d, gate, norm_post, w_mla, w_rwkv, w_gla, w_out)


def _rot_half_cols(w):
    half = w.shape[-1] // 2
    return jnp.concatenate([-w[..., half:], w[..., :half]], axis=-1)


def _permute_w_in(w):
    sizes = (512, 256, 64, 1024, 1024, 1024, 1024, 64, 64, 1024, 512, 512, 1024, 16, 1024,
             1024, 1024, 1024)
    parts = []
    start = 0
    for size in sizes:
        parts.append(w[:, start:start + size])
        start += size
    (c_q, c_kv, k_rope, mla_z, r, k, v, w_l, a_l, r_z, g_q, g_k, g_v, g_l, g_z,
     gate_a, gate_b, gate_c) = parts
    pad = jnp.zeros((w.shape[0], OFF_LORA - OFF_SLAB - MLA_ROPE - g_l.shape[1]), w.dtype)
    return jnp.concatenate([c_q, c_kv, k_rope, g_l, pad, w_l, a_l, mla_z, r, k, v, r_z,
                            g_q, g_k, g_v, g_z, gate_a, gate_b, gate_c], axis=1).astype(bf16)


def _mla_weights(w_uq, w_ukv):
    n_in = w_uq.shape[0]
    wq = w_uq.reshape(n_in, MLA_HEADS, MLA_NOPE + MLA_ROPE)
    wq = jnp.concatenate([wq, _rot_half_cols(wq[..., MLA_NOPE:])], axis=-1)
    wkv = w_ukv.reshape(w_ukv.shape[0], MLA_HEADS, MLA_NOPE + MLA_V)
    return wq.transpose(1, 0, 2).astype(bf16), wkv.transpose(1, 0, 2).astype(bf16)


def kernel(x, c, positions, ada_w, ada_b, norm_pre, norm_post, w_in, rwkv_mu, mla_q_norm,
           mla_kv_norm, mla_w_uq, mla_w_ukv, mla_w_o, rwkv_w0, rwkv_w2, rwkv_a0, rwkv_a2,
           rwkv_k_k, rwkv_k_a, rwkv_r_k, rwkv_ln_w, rwkv_ln_b, rwkv_w_o, gla_w2, gla_b,
           gla_norm, gla_w_o, w_out):
    batch, seq, _ = x.shape
    depth = w_in.shape[0]
    tokens = batch * seq
    x2d = x.reshape(tokens, D)
    pos2d = positions.reshape(tokens, 1)
    half = MLA_ROPE // 2
    inv_freq = ROPE_BASE ** (-jnp.arange(0, MLA_ROPE, 2, dtype=f32) / MLA_ROPE)
    invf = jnp.tile(inv_freq, LANES // half).reshape(1, LANES)
    idx = jnp.arange(QUAD)
    seg = (idx[:, None] // RWKV_HEAD == idx[None, :] // RWKV_HEAD).astype(f32)
    tri = jnp.arange(CHUNK)
    ltri = (tri[None, :] <= tri[:, None]).astype(f32)
    row1 = lambda t: t.reshape(1, -1)

    mod = _modulation(c, ada_w, ada_b)
    for l in range(depth):
        shift = mod[l, :, :D].reshape(batch, 1, D)
        scale = mod[l, :, D:2 * D].reshape(batch, 1, D)
        gate = mod[l, :, 2 * D:].reshape(batch, 1, D)
        p = _inproj(x2d, scale, shift, row1(norm_pre[l]), _permute_w_in(w_in[l]), seq)

        wq, wkv = _mla_weights(mla_w_uq[l], mla_w_ukv[l])
        q, k, v = _mla_prep(p, pos2d, invf, row1(mla_q_norm[l]), row1(mla_kv_norm[l]), wq, wkv,
                            batch, seq)
        o_mla = _flash(q, k, v, p, batch, seq).reshape(tokens, D)

        mu = rwkv_mu[l]
        mu_rkv = jnp.zeros((8, D), f32).at[:3].set(mu[:3 * D].reshape(3, D))
        w_lora = jnp.zeros((LANES, 2 * D), f32)
        w_lora = w_lora.at[:LANES // 2, :D].set(rwkv_w2[l]).at[LANES // 2:, D:].set(rwkv_a2[l])
        o_rwkv = _rwkv(p, mu_rkv, row1(mu[3 * D:]), row1(rwkv_w0[l]), row1(rwkv_a0[l]),
                       w_lora.astype(bf16), row1(rwkv_k_k[l]), row1(rwkv_k_a[l]),
                       row1(rwkv_r_k[l]), row1(rwkv_ln_w[l]), row1(rwkv_ln_b[l]), seg, ltri,
                       batch, seq)

        w2_pad = jnp.zeros((LANES, GLA_HEADS * GLA_DK), f32)
        w2_pad = w2_pad.at[OFF_GL:OFF_GL + gla_w2.shape[1]].set(gla_w2[l]).astype(bf16)
        o_gla = _gla(p, w2_pad, row1(gla_b[l]), row1(gla_norm[l]), ltri, batch, seq)

        x2d = _merge(o_mla, o_rwkv, o_gla, p, x2d, gate, row1(norm_post[l]),
                     mla_w_o[l].astype(bf16), rwkv_w_o[l].astype(bf16),
                     gla_w_o[l].astype(bf16), w_out[l].astype(bf16), seq)
    return x2d.reshape(batch, seq, D)
```

```python
import functools

import jax
import jax.numpy as jnp
from jax import lax
from jax.experimental import pallas as pl
from jax.experimental.pallas import tpu as pltpu

f32 = jnp.float32
bf16 = jnp.bfloat16

D = 1024
CHUNK = 64
NORM_EPS = 1e-6
MLA_HEADS = 8
MLA_NOPE = 128
MLA_ROPE = 64
MLA_V = 128
ROPE_BASE = 10000.0
LOG2_E = 1.4426950408889634
RWKV_HEAD = 64
RWKV_GN_EPS = 64e-5
GLA_HEADS = 4
GLA_DK = 128
GLA_DV = 256
GLA_GATE_NORM = 16.0
LANES = 128
QUAD = 256
N_QUADS = D // QUAD
P_COLS = 12 * D
VMEM_LIMIT = 56 * 1024 * 1024

T_SMALL, T_MLA_Z, T_R, T_K, T_V, T_RZ, T_GQK, T_GV, T_GZ, T_GA, T_GB, T_GC = range(12)
OFF_CKV = 512
OFF_SLAB = 768
OFF_GL = 64
OFF_LORA = 896

NN_DIMS = (((1,), (0,)), ((), ()))
NT_DIMS = (((1,), (1,)), ((), ()))
TN_DIMS = (((0,), (0,)), ((), ()))


def _div_pow2(x, n):
    return lax.shift_right_logical(x, jnp.int32(n.bit_length() - 1))


def _softplus(x):
    return jnp.maximum(x, 0.0) + jnp.log(1.0 + jnp.exp(-jnp.abs(x)))


def _sigmoid(x):
    return 1.0 / (1.0 + jnp.exp(-x))


def _silu(x):
    return x * _sigmoid(x)


def _split_bf16(x):
    hi = x.astype(bf16)
    return hi, (x - hi.astype(f32)).astype(bf16)


def _seg_sums(xs, seg):
    rows = xs[0].shape[0]
    out = jnp.dot(jnp.concatenate(xs, axis=0).astype(bf16), seg, preferred_element_type=f32)
    return [out[i * rows:(i + 1) * rows] for i in range(len(xs))]


def _chunk_cumsum(ltri, x):
    hi = x.astype(bf16)
    mid, lo = _split_bf16(x - hi.astype(f32))
    dot = lambda piece: jnp.dot(ltri, piece, preferred_element_type=f32)
    return dot(hi) + dot(mid) + dot(lo)


def _rms(x, gain):
    return x * lax.rsqrt(jnp.mean(x * x, axis=-1, keepdims=True) + NORM_EPS) * gain


def _mod_kernel(c_ref, w_ref, b_ref, o_ref):
    c = _silu(c_ref[...]).astype(bf16)
    o_ref[0] = jnp.dot(c, w_ref[0].astype(bf16), preferred_element_type=f32) + b_ref[0]


def _modulation(c, ada_w, ada_b):
    depth, _, n3 = ada_w.shape
    batch = c.shape[0]
    return pl.pallas_call(
        _mod_kernel,
        grid=(depth, n3 // D),
        in_specs=[pl.BlockSpec((batch, D), lambda l, j: (0, 0)),
                  pl.BlockSpec((1, D, D), lambda l, j: (l, 0, j)),
                  pl.BlockSpec((1, 1, D), lambda l, j: (l, 0, j))],
        out_specs=pl.BlockSpec((1, batch, D), lambda l, j: (l, 0, j)),
        out_shape=jax.ShapeDtypeStruct((depth, batch, n3), f32),
        name="adaln_mod",
    )(c, ada_w, ada_b.reshape(depth, 1, n3))


def _inproj_kernel(x_ref, scale_ref, shift_ref, g_ref, w_ref, o_ref, h_ref):
    @pl.when(pl.program_id(1) == 0)
    def _():
        h = _rms(x_ref[...], g_ref[...]) * (1.0 + scale_ref[0]) + shift_ref[0]
        h_ref[...] = h.astype(bf16)

    o_ref[...] = jnp.dot(h_ref[...], w_ref[...], preferred_element_type=f32).astype(bf16)


def _inproj(x2d, scale, shift, gain, w, seq):
    tokens = x2d.shape[0]
    tm = min(2048, seq)
    tn = D
    per_b = seq // tm
    return pl.pallas_call(
        _inproj_kernel,
        grid=(tokens // tm, P_COLS // tn),
        in_specs=[pl.BlockSpec((tm, D), lambda i, j: (i, 0)),
                  pl.BlockSpec((1, 1, D), lambda i, j: (i // per_b, 0, 0)),
                  pl.BlockSpec((1, 1, D), lambda i, j: (i // per_b, 0, 0)),
                  pl.BlockSpec((1, D), lambda i, j: (0, 0)),
                  pl.BlockSpec((D, tn), lambda i, j: (0, j))],
        out_specs=pl.BlockSpec((tm, tn), lambda i, j: (i, j)),
        out_shape=jax.ShapeDtypeStruct((tokens, P_COLS), bf16),
        scratch_shapes=[pltpu.VMEM((tm, D), bf16)],
        compiler_params=pltpu.CompilerParams(
            dimension_semantics=("parallel", "arbitrary"), vmem_limit_bytes=VMEM_LIMIT),
        name="norm_inproj",
    )(x2d, scale, shift, gain, w)


def _mla_prep_kernel(p_ref, pos_ref, invf_ref, qn_ref, kvn_ref, wq_ref, wkv_ref,
                     q_ref, k_ref, v_ref):
    p = p_ref[...].astype(f32)
    tm = p.shape[0]
    cq = _rms(p[:, :OFF_CKV], qn_ref[...]).astype(bf16)
    ckv = _rms(p[:, OFF_CKV:OFF_SLAB], kvn_ref[...]).astype(bf16)
    slab = p[:, OFF_SLAB:OFF_SLAB + LANES]
    ang = pos_ref[...].astype(f32) * invf_ref[...]
    cos = jnp.cos(ang)
    sin = jnp.sin(ang)
    lane = lax.broadcasted_iota(jnp.int32, (tm, LANES), 1)
    half = MLA_ROPE // 2
    rot = jnp.where(lane < half, -pltpu.roll(slab, LANES - half, 1), pltpu.roll(slab, half, 1))
    k_rope = (slab * cos + rot * sin)[:, :MLA_ROPE].astype(bf16)
    cos_sin = jnp.where(lane < MLA_ROPE, cos, sin)
    scale = (MLA_NOPE + MLA_ROPE) ** -0.5 * LOG2_E
    for h in range(MLA_HEADS):
        rq = jnp.dot(cq, wq_ref[h], preferred_element_type=f32)
        a = rq[:, MLA_NOPE:] * cos_sin
        q_rope = a + pltpu.roll(a, MLA_ROPE, 1)
        q_ref[0, h, :, :MLA_NOPE] = (rq[:, :MLA_NOPE] * scale).astype(bf16)
        q_ref[0, h, :, MLA_NOPE:] = (q_rope[:, :MLA_ROPE] * scale).astype(bf16)
        rkv = jnp.dot(ckv, wkv_ref[h], preferred_element_type=f32)
        k_ref[0, h, :, :MLA_NOPE] = rkv[:, :MLA_NOPE].astype(bf16)
        k_ref[0, h, :, MLA_NOPE:] = k_rope
        v_ref[0, h] = rkv[:, MLA_NOPE:].astype(bf16)


def _mla_prep(p, pos2d, invf, q_norm, kv_norm, wq, wkv, batch, seq):
    tm = min(512, seq)
    per_b = seq // tm
    dqk = MLA_NOPE + MLA_ROPE
    const2 = lambda b, i: (0, 0)
    const3 = lambda b, i: (0, 0, 0)
    return pl.pallas_call(
        _mla_prep_kernel,
        grid=(batch, per_b),
        in_specs=[pl.BlockSpec((tm, D), lambda b, i: (b * per_b + i, T_SMALL)),
                  pl.BlockSpec((tm, 1), lambda b, i: (b * per_b + i, 0)),
                  pl.BlockSpec((1, LANES), const2),
                  pl.BlockSpec((1, OFF_CKV), const2),
                  pl.BlockSpec((1, OFF_SLAB - OFF_CKV), const2),
                  pl.BlockSpec(wq.shape, const3),
                  pl.BlockSpec(wkv.shape, const3)],
        out_specs=[pl.BlockSpec((1, MLA_HEADS, tm, dqk), lambda b, i: (b, 0, i, 0)),
                   pl.BlockSpec((1, MLA_HEADS, tm, dqk), lambda b, i: (b, 0, i, 0)),
                   pl.BlockSpec((1, MLA_HEADS, tm, MLA_V), lambda b, i: (b, 0, i, 0))],
        out_shape=[jax.ShapeDtypeStruct((batch, MLA_HEADS, seq, dqk), bf16),
                   jax.ShapeDtypeStruct((batch, MLA_HEADS, seq, dqk), bf16),
                   jax.ShapeDtypeStruct((batch, MLA_HEADS, seq, MLA_V), bf16)],
        compiler_params=pltpu.CompilerParams(
            dimension_semantics=("parallel", "parallel"), vmem_limit_bytes=VMEM_LIMIT),
        name="mla_prep",
    )(p, pos2d, invf, q_norm, kv_norm, wq, wkv)


def _flash_kernel(q_ref, k_ref, v_ref, z_ref, o_ref, m_ref, l_ref, acc_ref, *, th):
    qi = pl.program_id(2)
    m_ref[...] = jnp.full(m_ref.shape, -jnp.inf, f32)
    l_ref[...] = jnp.zeros(l_ref.shape, f32)
    acc_ref[...] = jnp.zeros(acc_ref.shape, f32)
    n_col = th // LANES

    def step(half, ki, masked):
        start = pl.multiple_of(ki * th, th)
        q = q_ref[0, 0, half * th:(half + 1) * th, :]
        k = k_ref[0, 0, pl.ds(start, th), :]
        v = v_ref[0, 0, pl.ds(start, th), :]
        s = lax.dot_general(q, k, NT_DIMS, preferred_element_type=f32)
        if masked:
            q_chunk = _div_pow2(lax.broadcasted_iota(jnp.int32, (th, th), 0), CHUNK)
            k_chunk = _div_pow2(lax.broadcasted_iota(jnp.int32, (th, th), 1), CHUNK)
            s = jnp.where(k_chunk <= q_chunk, s, -jnp.inf)
        cols = [s[:, j * LANES:(j + 1) * LANES] for j in range(n_col)]
        col_max = functools.reduce(jnp.maximum, cols)
        m_prev = m_ref[half]
        m_new = jnp.maximum(m_prev, jnp.max(col_max, axis=-1, keepdims=True))
        alpha = jnp.exp2(m_prev - m_new)
        probs = [jnp.exp2(c - m_new) for c in cols]
        l_ref[half] = alpha * l_ref[half] + functools.reduce(jnp.add, probs)
        prob = jnp.concatenate(probs, axis=1).astype(bf16)
        acc_ref[half] = alpha * acc_ref[half] + jnp.dot(prob, v, preferred_element_type=f32)
        m_ref[half] = m_new

    n_full = 2 * qi

    def body(ki, carry):
        step(0, ki, False)
        step(1, ki, False)
        return carry

    lax.fori_loop(0, n_full, body, 0)
    step(0, n_full, True)
    step(1, n_full, False)
    step(1, n_full + 1, True)
    for half in range(2):
        rows = slice(half * th, (half + 1) * th)
        o = acc_ref[half] / jnp.sum(l_ref[half], axis=-1, keepdims=True)
        o_ref[0, rows, :] = (o * _silu(z_ref[rows, :].astype(f32))).astype(bf16)


def _flash(q, k, v, p, batch, seq):
    th = min(512, seq // 2)
    tq = 2 * th
    nq = seq // tq
    dqk = MLA_NOPE + MLA_ROPE
    z_col0 = T_MLA_Z * (D // MLA_V)
    return pl.pallas_call(
        functools.partial(_flash_kernel, th=th),
        grid=(batch, MLA_HEADS, nq),
        in_specs=[pl.BlockSpec((1, 1, tq, dqk), lambda b, h, i: (b, h, i, 0)),
                  pl.BlockSpec((1, 1, seq, dqk), lambda b, h, i: (b, h, 0, 0)),
                  pl.BlockSpec((1, 1, seq, MLA_V), lambda b, h, i: (b, h, 0, 0)),
                  pl.BlockSpec((tq, MLA_V), lambda b, h, i: (b * nq + i, z_col0 + h))],
        out_specs=pl.BlockSpec((1, tq, MLA_V), lambda b, h, i: (b, i, h)),
        out_shape=jax.ShapeDtypeStruct((batch, seq, MLA_HEADS * MLA_V), bf16),
        scratch_shapes=[pltpu.VMEM((2, th, LANES), f32), pltpu.VMEM((2, th, LANES), f32),
                        pltpu.VMEM((2, th, MLA_V), f32)],
        compiler_params=pltpu.CompilerParams(
            dimension_semantics=("parallel", "parallel", "arbitrary"),
            vmem_limit_bytes=VMEM_LIMIT),
        name="mla_flash",
    )(q, k, v, p)


def _shift_mix(s, prev_row, mu):
    first = lax.broadcasted_iota(jnp.int32, s.shape, 0) == 0
    prev = jnp.where(first, prev_row, pltpu.roll(s, 1, 0))
    return s + mu * (prev - s)


def _rwkv_kernel(r_ref, k_ref, v_ref, lora_ref, z_ref, mu_ref, mul_ref, w0_ref, a0_ref, wl_ref,
                 kk_ref, ka_ref, rk_ref, lnw_ref, lnb_ref, seg_ref, ltri_ref,
                 o_ref, state_ref, prev_ref, prevl_ref):
    c_len = r_ref.shape[0]

    @pl.when(pl.program_id(1) == 0)
    def _():
        state_ref[...] = jnp.zeros(state_ref.shape, f32)
        prev_ref[...] = jnp.zeros(prev_ref.shape, f32)
        prevl_ref[...] = jnp.zeros(prevl_ref.shape, f32)

    r_raw, k_raw, v_raw, lora_raw = (
        ref[...].astype(f32) for ref in (r_ref, k_ref, v_ref, lora_ref))
    r = _shift_mix(r_raw, prev_ref[0:1, :], mu_ref[0:1, :])
    k = _shift_mix(k_raw, prev_ref[1:2, :], mu_ref[1:2, :])
    v = _shift_mix(v_raw, prev_ref[2:3, :], mu_ref[2:3, :])
    lora = _shift_mix(lora_raw, prevl_ref[0:1, :], mul_ref[...])
    prev_ref[0:1, :] = r_raw[c_len - 1:c_len, :]
    prev_ref[1:2, :] = k_raw[c_len - 1:c_len, :]
    prev_ref[2:3, :] = v_raw[c_len - 1:c_len, :]
    prevl_ref[0:1, :] = lora_raw[c_len - 1:c_len, :]

    lane = lax.broadcasted_iota(jnp.int32, lora.shape, 1)
    lora = jnp.where(lane < LANES // 2, jnp.tanh(lora), lora)
    proj = jnp.dot(lora.astype(bf16), wl_ref[...], preferred_element_type=f32)
    w = -_softplus(-(w0_ref[...] + proj[:, :D])) - 0.5
    log_decay = -jnp.exp(w)
    a = _sigmoid(a0_ref[...] + proj[:, D:])
    kk = k * kk_ref[...]
    k = k * (1.0 + (a - 1.0) * ka_ref[...])

    cum = _chunk_cumsum(ltri_ref[...], log_decay)
    cum_last = cum[c_len - 1:c_len, :]
    g_inc = jnp.exp(cum)
    g_inv = jnp.exp(-cum)
    g_exc = jnp.exp(cum - log_decay)
    g_end = jnp.exp(cum_last)

    seg = seg_ref[...]
    row = lax.broadcasted_iota(jnp.int32, (c_len, QUAD), 0)
    col = lax.broadcasted_iota(jnp.int32, (c_len, QUAD), 1)
    col_t = jnp.bitwise_and(col, c_len - 1)
    col_head = _div_pow2(col, RWKV_HEAD)
    strict = col_t < row
    incl = col_t <= row
    eye = (col_t == row).astype(f32)
    head_masks = [(col_head == g).astype(bf16) for g in range(QUAD // RWKV_HEAD)]
    block_diag = (_div_pow2(lax.broadcasted_iota(jnp.int32, (QUAD, QUAD), 0), RWKV_HEAD)
                  == _div_pow2(lax.broadcasted_iota(jnp.int32, (QUAD, QUAD), 1), RWKV_HEAD))

    def stack(y):
        y = y.astype(bf16)
        return jnp.concatenate([y * m for m in head_masks], axis=0)

    def bdot(x, y, dims=NN_DIMS):
        return lax.dot_general(x.astype(bf16), y.astype(bf16), dims, preferred_element_type=f32)

    def qprod(x, y):
        return bdot(x, stack(y))

    slabs = [slice(qd * QUAD, (qd + 1) * QUAD) for qd in range(N_QUADS)]
    each = lambda fn, *cols: [fn(*args) for args in zip(*cols)]
    cut = lambda x: [x[:, sl] for sl in slabs]

    kk_q = cut(kk)
    kk_sq = _seg_sums([x * x for x in kk_q], seg)
    kk_q = each(lambda x, sq: x / jnp.maximum(jnp.sqrt(sq), 1e-12), kk_q, kk_sq)
    a_bar = each(lambda x, g: -x * g, kk_q, cut(g_exc))
    b_til = each(lambda x, a_q, g: x * a_q * g, kk_q, cut(a), cut(g_inv))
    k_til = each(jnp.multiply, cut(k), cut(g_inv))
    r_bar = each(jnp.multiply, cut(r), cut(g_inc))
    g_end_q = cut(g_end)
    b_end = each(jnp.multiply, b_til, g_end_q)
    k_end = each(jnp.multiply, k_til, g_end_q)
    v_q = cut(v)

    pair = each(lambda ab, rb, bt, kt: bdot(jnp.concatenate([ab, rb], axis=0),
                                            jnp.concatenate([stack(bt), stack(kt)], axis=0),
                                            NT_DIMS),
                a_bar, r_bar, b_til, k_til)
    a_ab = [jnp.where(strict, x[:c_len, :QUAD], 0.0) for x in pair]
    a_ak = [jnp.where(strict, x[:c_len, QUAD:], 0.0) for x in pair]
    a_rb = [jnp.where(incl, x[c_len:, :QUAD], 0.0) for x in pair]
    a_rk = [jnp.where(incl, x[c_len:, QUAD:], 0.0) for x in pair]

    t_inv = [eye + x for x in a_ab]
    power = a_ab
    span = 2
    while span < c_len:
        power = each(qprod, power, power)
        t_inv = each(lambda t, pw: t + qprod(t, pw), t_inv, power)
        span *= 2

    w_mat = each(qprod, t_inv, a_bar)
    u_loc = each(qprod, t_inv, each(qprod, a_ak, v_q))
    state = [state_ref[qd] for qd in range(N_QUADS)]
    u = each(lambda wm, st, ul: bdot(wm, st, NT_DIMS) + ul, w_mat, state, u_loc)
    y = each(lambda rb, st, arb, u_q, ark, vq: (bdot(rb, st, NT_DIMS) + qprod(arb, u_q)
                                                + qprod(ark, vq)),
             r_bar, state, a_rb, u, a_rk, v_q)
    upd = each(lambda u_q, vq, be, ke: bdot(jnp.concatenate([u_q, vq], axis=0),
                                            jnp.concatenate([be, ke], axis=0), TN_DIMS),
               u, v_q, b_end, k_end)
    for qd in range(N_QUADS):
        state_ref[qd] = state[qd] * g_end_q[qd] + jnp.where(block_diag, upd[qd], 0.0)

    inv_n = 1.0 / RWKV_HEAD
    rkr = each(lambda r_q, k_q, rk_q: r_q * k_q * rk_q, cut(r), cut(k), cut(rk_ref[...]))
    sums = _seg_sums(y + rkr, seg)
    dev = each(lambda y_q, total: y_q - total * inv_n, y, sums[:N_QUADS])
    bonus = each(jnp.multiply, sums[N_QUADS:], v_q)
    var = [total * inv_n for total in _seg_sums([d_q * d_q for d_q in dev], seg)]
    for qd, sl in enumerate(slabs):
        y_q = dev[qd] * lax.rsqrt(var[qd] + RWKV_GN_EPS) * lnw_ref[:, sl] + lnb_ref[:, sl]
        o_ref[:, sl] = ((y_q + bonus[qd]) * _silu(z_ref[:, sl].astype(f32))).astype(bf16)


def _rwkv(p, mu_rkv, mu_lora, w0, a0, w_lora, k_k, k_a, r_k, ln_w, ln_b, seg, ltri, batch, seq):
    nc = seq // CHUNK
    tokens = batch * seq
    lora_col = T_SMALL * (D // LANES) + OFF_LORA // LANES
    tile = lambda t: pl.BlockSpec((CHUNK, D), lambda b, c: (b * nc + c, t))
    const = lambda shape: pl.BlockSpec(shape, lambda b, c: (0, 0))
    return pl.pallas_call(
        _rwkv_kernel,
        grid=(batch, nc),
        in_specs=[tile(T_R), tile(T_K), tile(T_V),
                  pl.BlockSpec((CHUNK, LANES), lambda b, c: (b * nc + c, lora_col)),
                  tile(T_RZ),
                  const(mu_rkv.shape), const(mu_lora.shape), const((1, D)), const((1, D)),
                  const(w_lora.shape),
                  const((1, D)), const((1, D)), const((1, D)), const((1, D)), const((1, D)),
                  const(seg.shape), const(ltri.shape)],
        out_specs=pl.BlockSpec((CHUNK, D), lambda b, c: (b * nc + c, 0)),
        out_shape=jax.ShapeDtypeStruct((tokens, D), bf16),
        scratch_shapes=[pltpu.VMEM((N_QUADS, QUAD, QUAD), f32),
                        pltpu.VMEM((8, D), f32), pltpu.VMEM((8, LANES), f32)],
        compiler_params=pltpu.CompilerParams(
            dimension_semantics=("parallel", "arbitrary"), vmem_limit_bytes=VMEM_LIMIT),
        name="rwkv7_chunk",
    )(p, p, p, p, p, mu_rkv, mu_lora, w0, a0, w_lora, k_k, k_a, r_k, ln_w, ln_b, seg, ltri)


def _gla_kernel(qk_ref, v_ref, gl_ref, z_ref, w2_ref, gb_ref, ng_ref, ltri_ref, o_ref, state_ref):
    c_len = qk_ref.shape[0]
    kw = GLA_HEADS * GLA_DK

    @pl.when(pl.program_id(1) == 0)
    def _():
        state_ref[...] = jnp.zeros(state_ref.shape, f32)

    qk = qk_ref[...].astype(f32)
    q = qk[:, :kw] * (GLA_DK ** -0.5)
    k = qk[:, kw:]
    gate = jnp.dot(gl_ref[...], w2_ref[...], preferred_element_type=f32) + gb_ref[...]
    log_a = -_softplus(-gate) / GLA_GATE_NORM
    cum = _chunk_cumsum(ltri_ref[...], log_a)
    cum_last = cum[c_len - 1:c_len, :]
    q_dec = (q * jnp.exp(cum)).astype(bf16)
    k_inv = (k * jnp.exp(-cum)).astype(bf16)
    k_end = (k * jnp.exp(cum_last - cum)).astype(bf16)
    decay = jnp.exp(cum_last)
    causal = (lax.broadcasted_iota(jnp.int32, (c_len, c_len), 1)
              <= lax.broadcasted_iota(jnp.int32, (c_len, c_len), 0))
    heads = range(GLA_HEADS)
    ks = [slice(h * GLA_DK, (h + 1) * GLA_DK) for h in heads]
    vs = [slice(h * GLA_DV, (h + 1) * GLA_DV) for h in heads]
    v_h = [v_ref[:, sl] for sl in vs]
    state = [state_ref[h] for h in heads]
    attn = [lax.dot_general(q_dec[:, sl], k_inv[:, sl], NT_DIMS, preferred_element_type=f32)
            for sl in ks]
    attn = [jnp.where(causal, x, 0.0).astype(bf16) for x in attn]
    o = [jnp.dot(attn[h], v_h[h], preferred_element_type=f32)
         + lax.dot_general(q_dec[:, ks[h]], state[h].astype(bf16), NT_DIMS,
                           preferred_element_type=f32) for h in heads]
    for h in heads:
        state_ref[h] = state[h] * decay[:, ks[h]] + lax.dot_general(
            v_h[h], k_end[:, ks[h]], TN_DIMS, preferred_element_type=f32)
    for h in heads:
        o_h = o[h] * lax.rsqrt(jnp.mean(o[h] * o[h], axis=-1, keepdims=True) + NORM_EPS)
        o_ref[:, vs[h]] = (o_h * ng_ref[...] * _silu(z_ref[:, vs[h]].astype(f32))).astype(bf16)


def _gla(p, w2_pad, g_bias, norm_g, ltri, batch, seq):
    nc = seq // CHUNK
    tokens = batch * seq
    slab_col = T_SMALL * (D // LANES) + OFF_SLAB // LANES
    tile = lambda t: pl.BlockSpec((CHUNK, D), lambda b, c: (b * nc + c, t))
    const = lambda shape: pl.BlockSpec(shape, lambda b, c: (0, 0))
    return pl.pallas_call(
        _gla_kernel,
        grid=(batch, nc),
        in_specs=[tile(T_GQK), tile(T_GV),
                  pl.BlockSpec((CHUNK, LANES), lambda b, c: (b * nc + c, slab_col)),
                  tile(T_GZ),
                  const(w2_pad.shape), const(g_bias.shape), const(norm_g.shape),
                  const(ltri.shape)],
        out_specs=pl.BlockSpec((CHUNK, D), lambda b, c: (b * nc + c, 0)),
        out_shape=jax.ShapeDtypeStruct((tokens, D), bf16),
        scratch_shapes=[pltpu.VMEM((GLA_HEADS, GLA_DV, GLA_DK), f32)],
        compiler_params=pltpu.CompilerParams(
            dimension_semantics=("parallel", "arbitrary"), vmem_limit_bytes=VMEM_LIMIT),
        name="gla_chunk",
    )(p, p, p, p, w2_pad, g_bias, norm_g, ltri)


def _merge_kernel(om_ref, or_ref, og_ref, ga_ref, gb_ref, gc_ref, x_ref, gate_ref, np_ref,
                  wm_ref, wr_ref, wg_ref, wo_ref, o_ref):
    gate_of = lambda ref: _sigmoid(ref[...].astype(f32))
    merged = (gate_of(ga_ref) * jnp.dot(om_ref[...], wm_ref[...], preferred_element_type=f32)
              + gate_of(gb_ref) * jnp.dot(or_ref[...], wr_ref[...], preferred_element_type=f32)
              + gate_of(gc_ref) * jnp.dot(og_ref[...], wg_ref[...], preferred_element_type=f32))
    y = jnp.dot(merged.astype(bf16), wo_ref[...], preferred_element_type=f32)
    o_ref[...] = x_ref[...] + gate_ref[0] * _rms(y, np_ref[...])


def _merge(o_mla, o_rwkv, o_gla, p, x2d, gate, norm_post, w_mla, w_rwkv, w_gla, w_out, seq):
    tokens = x2d.shape[0]
    tm = min(512, seq)
    per_b = seq // tm
    row = lambda i: (i, 0)
    const = lambda i: (0, 0)
    return pl.pallas_call(
        _merge_kernel,
        grid=(tokens // tm,),
        in_specs=[pl.BlockSpec((tm, D), row), pl.BlockSpec((tm, D), row),
                  pl.BlockSpec((tm, D), row),
                  pl.BlockSpec((tm, D), lambda i: (i, T_GA)),
                  pl.BlockSpec((tm, D), lambda i: (i, T_GB)),
                  pl.BlockSpec((tm, D), lambda i: (i, T_GC)),
                  pl.BlockSpec((tm, D), row),
                  pl.BlockSpec((1, 1, D), lambda i: (i // per_b, 0, 0)),
                  pl.BlockSpec((1, D), const),
                  pl.BlockSpec((D, D), const), pl.BlockSpec((D, D), const),
                  pl.BlockSpec((D, D), const), pl.BlockSpec((D, D), const)],
        out_specs=pl.BlockSpec((tm, D), row),
        out_shape=jax.ShapeDtypeStruct((tokens, D), f32),
        compiler_params=pltpu.CompilerParams(
            dimension_semantics=("parallel",), vmem_limit_bytes=VMEM_LIMIT),
        name="merge_outproj",
    )(o_mla, o_rwkv, o_gla, p, p, p, x2d, gate, norm_post, w_mla, w_rwkv, w_gla, w_out)


def _rot_half_cols(w):
    half = w.shape[-1] // 2
    return jnp.concatenate([-w[..., half:], w[..., :half]], axis=-1)


def _permute_w_in(w):
    sizes = (512, 256, 64, 1024, 1024, 1024, 1024, 64, 64, 1024, 512, 512, 1024, 16, 1024,
             1024, 1024, 1024)
    parts = []
    start = 0
    for size in sizes:
        parts.append(w[:, start:start + size])
        start += size
    (c_q, c_kv, k_rope, mla_z, r, k, v, w_l, a_l, r_z, g_q, g_k, g_v, g_l, g_z,
     gate_a, gate_b, gate_c) = parts
    pad = jnp.zeros((w.shape[0], OFF_LORA - OFF_SLAB - MLA_ROPE - g_l.shape[1]), w.dtype)
    return jnp.concatenate([c_q, c_kv, k_rope, g_l, pad, w_l, a_l, mla_z, r, k, v, r_z,
                            g_q, g_k, g_v, g_z, gate_a, gate_b, gate_c], axis=1).astype(bf16)


def _mla_weights(w_uq, w_ukv):
    n_in = w_uq.shape[0]
    wq = w_uq.reshape(n_in, MLA_HEADS, MLA_NOPE + MLA_ROPE)
    wq = jnp.concatenate([wq, _rot_half_cols(wq[..., MLA_NOPE:])], axis=-1)
    wkv = w_ukv.reshape(w_ukv.shape[0], MLA_HEADS, MLA_NOPE + MLA_V)
    return wq.transpose(1, 0, 2).astype(bf16), wkv.transpose(1, 0, 2).astype(bf16)


def kernel(x, c, positions, ada_w, ada_b, norm_pre, norm_post, w_in, rwkv_mu, mla_q_norm,
           mla_kv_norm, mla_w_uq, mla_w_ukv, mla_w_o, rwkv_w0, rwkv_w2, rwkv_a0, rwkv_a2,
           rwkv_k_k, rwkv_k_a, rwkv_r_k, rwkv_ln_w, rwkv_ln_b, rwkv_w_o, gla_w2, gla_b,
           gla_norm, gla_w_o, w_out):
    batch, seq, _ = x.shape
    depth = w_in.shape[0]
    tokens = batch * seq
    x2d = x.reshape(tokens, D)
    pos2d = positions.reshape(tokens, 1)
    half = MLA_ROPE // 2
    inv_freq = ROPE_BASE ** (-jnp.arange(0, MLA_ROPE, 2, dtype=f32) / MLA_ROPE)
    invf = jnp.tile(inv_freq, LANES // half).reshape(1, LANES)
    idx = jnp.arange(QUAD)
    seg = (idx[:, None] // RWKV_HEAD == idx[None, :] // RWKV_HEAD).astype(bf16)
    tri = jnp.arange(CHUNK)
    ltri = (tri[None, :] <= tri[:, None]).astype(bf16)
    row1 = lambda t: t.reshape(1, -1)

    mod = _modulation(c, ada_w, ada_b)
    for l in range(depth):
        shift = mod[l, :, :D].reshape(batch, 1, D)
        scale = mod[l, :, D:2 * D].reshape(batch, 1, D)
        gate = mod[l, :, 2 * D:].reshape(batch, 1, D)
        p = _inproj(x2d, scale, shift, row1(norm_pre[l]), _permute_w_in(w_in[l]), seq)

        wq, wkv = _mla_weights(mla_w_uq[l], mla_w_ukv[l])
        q, k, v = _mla_prep(p, pos2d, invf, row1(mla_q_norm[l]), row1(mla_kv_norm[l]), wq, wkv,
                            batch, seq)
        o_mla = _flash(q, k, v, p, batch, seq).reshape(tokens, D)

        mu = rwkv_mu[l]
        mu_rkv = jnp.zeros((8, D), f32).at[:3].set(mu[:3 * D].reshape(3, D))
        w_lora = jnp.zeros((LANES, 2 * D), f32)
        w_lora = w_lora.at[:LANES // 2, :D].set(rwkv_w2[l]).at[LANES // 2:, D:].set(rwkv_a2[l])
        o_rwkv = _rwkv(p, mu_rkv, row1(mu[3 * D:]), row1(rwkv_w0[l]), row1(rwkv_a0[l]),
                       w_lora.astype(bf16), row1(rwkv_k_k[l]), row1(rwkv_k_a[l]),
                       row1(rwkv_r_k[l]), row1(rwkv_ln_w[l]), row1(rwkv_ln_b[l]), seg, ltri,
                       batch, seq)

        w2_pad = jnp.zeros((LANES, GLA_HEADS * GLA_DK), f32)
        w2_pad = w2_pad.at[OFF_GL:OFF_GL + gla_w2.shape[1]].set(gla_w2[l]).astype(bf16)
        o_gla = _gla(p, w2_pad, row1(gla_b[l]), row1(gla_norm[l]), ltri, batch, seq)

        x2d = _merge(o_mla, o_rwkv, o_gla, p, x2d, gate, row1(norm_post[l]),
                     mla_w_o[l].astype(bf16), rwkv_w_o[l].astype(bf16),
                     gla_w_o[l].astype(bf16), w_out[l].astype(bf16), seq)
    return x2d.reshape(batch, seq, D)
```

```python
import functools

import jax
import jax.numpy as jnp
from jax import lax
from jax.experimental import pallas as pl
from jax.experimental.pallas import tpu as pltpu

f32 = jnp.float32
bf16 = jnp.bfloat16

D = 1024
CHUNK = 64
PAIR = 2
NORM_EPS = 1e-6
MLA_HEADS = 8
MLA_NOPE = 128
MLA_ROPE = 64
MLA_V = 128
FLASH_HEADS = 2
ROPE_BASE = 10000.0
LOG2_E = 1.4426950408889634
RWKV_HEAD = 64
RWKV_GN_EPS = 64e-5
GLA_HEADS = 4
GLA_DK = 128
GLA_DV = 256
GLA_GATE_NORM = 16.0
LANES = 128
QUAD = 128
N_QUADS = D // QUAD
P_COLS = 12 * D
VMEM_LIMIT = 56 * 1024 * 1024

T_SMALL, T_MLA_Z, T_R, T_K, T_V, T_RZ, T_GQK, T_GV, T_GZ, T_GA, T_GB, T_GC = range(12)
OFF_CKV = 512
OFF_SLAB = 768
OFF_GL = 64
OFF_LORA = 896

NN_DIMS = (((1,), (0,)), ((), ()))
NT_DIMS = (((1,), (1,)), ((), ()))
TN_DIMS = (((0,), (0,)), ((), ()))


def _div_pow2(x, n):
    return lax.shift_right_logical(x, jnp.int32(n.bit_length() - 1))


def _softplus(x):
    return jnp.maximum(x, 0.0) + jnp.log(1.0 + jnp.exp(-jnp.abs(x)))


def _sigmoid(x):
    return 1.0 / (1.0 + jnp.exp(-x))


def _silu(x):
    return x * _sigmoid(x)


def _split_bf16(x):
    hi = x.astype(bf16)
    return hi, (x - hi.astype(f32)).astype(bf16)


def _seg_sums(xs, seg):
    rows = xs[0].shape[0]
    out = jnp.dot(jnp.concatenate(xs, axis=0).astype(bf16), seg, preferred_element_type=f32)
    return [out[i * rows:(i + 1) * rows] for i in range(len(xs))]


def _chunk_cumsum(ltri, x):
    hi, lo = _split_bf16(x)
    return (jnp.dot(ltri, hi, preferred_element_type=f32)
            + jnp.dot(ltri, lo, preferred_element_type=f32))


def _rms(x, gain):
    return x * lax.rsqrt(jnp.mean(x * x, axis=-1, keepdims=True) + NORM_EPS) * gain


def _mod_kernel(c_ref, w_ref, b_ref, o_ref):
    c = _silu(c_ref[...]).astype(bf16)
    o_ref[0] = jnp.dot(c, w_ref[0].astype(bf16), preferred_element_type=f32) + b_ref[0]


def _modulation(c, ada_w, ada_b):
    depth, _, n3 = ada_w.shape
    batch = c.shape[0]
    return pl.pallas_call(
        _mod_kernel,
        grid=(depth, n3 // D),
        in_specs=[pl.BlockSpec((batch, D), lambda l, j: (0, 0)),
                  pl.BlockSpec((1, D, D), lambda l, j: (l, 0, j)),
                  pl.BlockSpec((1, 1, D), lambda l, j: (l, 0, j))],
        out_specs=pl.BlockSpec((1, batch, D), lambda l, j: (l, 0, j)),
        out_shape=jax.ShapeDtypeStruct((depth, batch, n3), f32),
        name="adaln_mod",
    )(c, ada_w, ada_b.reshape(depth, 1, n3))


def _inproj_kernel(x_ref, scale_ref, shift_ref, g_ref, w_ref, o_ref, h_ref):
    @pl.when(pl.program_id(1) == 0)
    def _():
        h = _rms(x_ref[...], g_ref[...]) * (1.0 + scale_ref[0]) + shift_ref[0]
        h_ref[...] = h.astype(bf16)

    o_ref[...] = jnp.dot(h_ref[...], w_ref[...], preferred_element_type=f32).astype(bf16)


def _inproj(x2d, scale, shift, gain, w, seq):
    tokens = x2d.shape[0]
    tm = min(2048, seq)
    tn = D
    per_b = seq // tm
    return pl.pallas_call(
        _inproj_kernel,
        grid=(tokens // tm, P_COLS // tn),
        in_specs=[pl.BlockSpec((tm, D), lambda i, j: (i, 0)),
                  pl.BlockSpec((1, 1, D), lambda i, j: (i // per_b, 0, 0)),
                  pl.BlockSpec((1, 1, D), lambda i, j: (i // per_b, 0, 0)),
                  pl.BlockSpec((1, D), lambda i, j: (0, 0)),
                  pl.BlockSpec((D, tn), lambda i, j: (0, j))],
        out_specs=pl.BlockSpec((tm, tn), lambda i, j: (i, j)),
        out_shape=jax.ShapeDtypeStruct((tokens, P_COLS), bf16),
        scratch_shapes=[pltpu.VMEM((tm, D), bf16)],
        compiler_params=pltpu.CompilerParams(
            dimension_semantics=("parallel", "arbitrary"), vmem_limit_bytes=VMEM_LIMIT),
        name="norm_inproj",
    )(x2d, scale, shift, gain, w)


def _mla_prep_kernel(p_ref, pos_ref, invf_ref, qn_ref, kvn_ref, wq_ref, wkv_ref,
                     q_ref, k_ref, v_ref):
    p = p_ref[...].astype(f32)
    tm = p.shape[0]
    cq = _rms(p[:, :OFF_CKV], qn_ref[...]).astype(bf16)
    ckv = _rms(p[:, OFF_CKV:OFF_SLAB], kvn_ref[...]).astype(bf16)
    slab = p[:, OFF_SLAB:OFF_SLAB + LANES]
    ang = pos_ref[...].astype(f32) * invf_ref[...]
    cos = jnp.cos(ang)
    sin = jnp.sin(ang)
    lane = lax.broadcasted_iota(jnp.int32, (tm, LANES), 1)
    half = MLA_ROPE // 2
    rot = jnp.where(lane < half, -pltpu.roll(slab, LANES - half, 1), pltpu.roll(slab, half, 1))
    k_rope = (slab * cos + rot * sin)[:, :MLA_ROPE].astype(bf16)
    cos_sin = jnp.where(lane < MLA_ROPE, cos, sin)
    scale = (MLA_NOPE + MLA_ROPE) ** -0.5 * LOG2_E
    for h in range(MLA_HEADS):
        rq = jnp.dot(cq, wq_ref[h], preferred_element_type=f32)
        a = rq[:, MLA_NOPE:] * cos_sin
        q_rope = a + pltpu.roll(a, MLA_ROPE, 1)
        q_ref[0, h, :, :MLA_NOPE] = (rq[:, :MLA_NOPE] * scale).astype(bf16)
        q_ref[0, h, :, MLA_NOPE:] = (q_rope[:, :MLA_ROPE] * scale).astype(bf16)
        rkv = jnp.dot(ckv, wkv_ref[h], preferred_element_type=f32)
        k_ref[0, h, :, :MLA_NOPE] = rkv[:, :MLA_NOPE].astype(bf16)
        k_ref[0, h, :, MLA_NOPE:] = k_rope
        v_ref[0, h] = rkv[:, MLA_NOPE:].astype(bf16)


def _mla_prep(p, pos2d, invf, q_norm, kv_norm, wq, wkv, batch, seq):
    tm = min(512, seq)
    per_b = seq // tm
    dqk = MLA_NOPE + MLA_ROPE
    const2 = lambda b, i: (0, 0)
    const3 = lambda b, i: (0, 0, 0)
    return pl.pallas_call(
        _mla_prep_kernel,
        grid=(batch, per_b),
        in_specs=[pl.BlockSpec((tm, D), lambda b, i: (b * per_b + i, T_SMALL)),
                  pl.BlockSpec((tm, 1), lambda b, i: (b * per_b + i, 0)),
                  pl.BlockSpec((1, LANES), const2),
                  pl.BlockSpec((1, OFF_CKV), const2),
                  pl.BlockSpec((1, OFF_SLAB - OFF_CKV), const2),
                  pl.BlockSpec(wq.shape, const3),
                  pl.BlockSpec(wkv.shape, const3)],
        out_specs=[pl.BlockSpec((1, MLA_HEADS, tm, dqk), lambda b, i: (b, 0, i, 0)),
                   pl.BlockSpec((1, MLA_HEADS, tm, dqk), lambda b, i: (b, 0, i, 0)),
                   pl.BlockSpec((1, MLA_HEADS, tm, MLA_V), lambda b, i: (b, 0, i, 0))],
        out_shape=[jax.ShapeDtypeStruct((batch, MLA_HEADS, seq, dqk), bf16),
                   jax.ShapeDtypeStruct((batch, MLA_HEADS, seq, dqk), bf16),
                   jax.ShapeDtypeStruct((batch, MLA_HEADS, seq, MLA_V), bf16)],
        compiler_params=pltpu.CompilerParams(
            dimension_semantics=("parallel", "parallel"), vmem_limit_bytes=VMEM_LIMIT),
        name="mla_prep",
    )(p, pos2d, invf, q_norm, kv_norm, wq, wkv)


def _flash_kernel(q_ref, k_ref, v_ref, z_ref, o_ref, m_ref, l_ref, acc_ref, *, th):
    qi = pl.program_id(2)
    m_ref[...] = jnp.full(m_ref.shape, -jnp.inf, f32)
    l_ref[...] = jnp.zeros(l_ref.shape, f32)
    acc_ref[...] = jnp.zeros(acc_ref.shape, f32)
    n_col = th // LANES

    def run(chains):
        slot = [2 * h + half for h, half, _, _ in chains]
        start = [pl.multiple_of(ki * th, th) for _, _, ki, _ in chains]
        s = [lax.dot_general(q_ref[0, h, half * th:(half + 1) * th, :],
                             k_ref[0, h, pl.ds(st, th), :], NT_DIMS, preferred_element_type=f32)
             for (h, half, _, _), st in zip(chains, start)]
        if any(diag for _, _, _, diag in chains):
            q_chunk = _div_pow2(lax.broadcasted_iota(jnp.int32, (th, th), 0), CHUNK)
            k_chunk = _div_pow2(lax.broadcasted_iota(jnp.int32, (th, th), 1), CHUNK)
            allowed = k_chunk <= q_chunk
            s = [jnp.where(allowed, x, -jnp.inf) if diag else x
                 for x, (_, _, _, diag) in zip(s, chains)]
        cols = [[x[:, j * LANES:(j + 1) * LANES] for j in range(n_col)] for x in s]
        m_prev = [m_ref[i] for i in slot]
        m_new = [jnp.maximum(mp, jnp.max(functools.reduce(jnp.maximum, c), axis=-1,
                                         keepdims=True)) for mp, c in zip(m_prev, cols)]
        alpha = [jnp.exp2(mp - mn) for mp, mn in zip(m_prev, m_new)]
        probs = [[jnp.exp2(x - mn) for x in c] for c, mn in zip(cols, m_new)]
        for i, al, pr in zip(slot, alpha, probs):
            l_ref[i] = al * l_ref[i] + functools.reduce(jnp.add, pr)
        pv = [jnp.dot(jnp.concatenate(pr, axis=1).astype(bf16), v_ref[0, h, pl.ds(st, th), :],
                      preferred_element_type=f32)
              for pr, (h, _, _, _), st in zip(probs, chains, start)]
        for i, al, x, mn in zip(slot, alpha, pv, m_new):
            acc_ref[i] = al * acc_ref[i] + x
            m_ref[i] = mn

    heads = range(FLASH_HEADS)
    n_full = 2 * qi

    def body(ki, carry):
        run([(h, half, ki, False) for h in heads for half in range(2)])
        return carry

    lax.fori_loop(0, n_full, body, 0)
    run([(h, half, n_full, half == 0) for h in heads for half in range(2)])
    run([(h, 1, n_full + 1, True) for h in heads])
    for h in heads:
        lanes = slice(h * MLA_V, (h + 1) * MLA_V)
        for half in range(2):
            rows = slice(half * th, (half + 1) * th)
            o = acc_ref[2 * h + half] / jnp.sum(l_ref[2 * h + half], axis=-1, keepdims=True)
            o_ref[0, rows, lanes] = (o * _silu(z_ref[rows, lanes].astype(f32))).astype(bf16)


def _flash(q, k, v, p, batch, seq):
    th = min(512, seq // 2)
    tq = 2 * th
    nq = seq // tq
    dqk = MLA_NOPE + MLA_ROPE
    hv = FLASH_HEADS * MLA_V
    z_col0 = T_MLA_Z * (D // hv)
    n_chain = 2 * FLASH_HEADS
    return pl.pallas_call(
        functools.partial(_flash_kernel, th=th),
        grid=(batch, MLA_HEADS // FLASH_HEADS, nq),
        in_specs=[pl.BlockSpec((1, FLASH_HEADS, tq, dqk), lambda b, h, i: (b, h, i, 0)),
                  pl.BlockSpec((1, FLASH_HEADS, seq, dqk), lambda b, h, i: (b, h, 0, 0)),
                  pl.BlockSpec((1, FLASH_HEADS, seq, MLA_V), lambda b, h, i: (b, h, 0, 0)),
                  pl.BlockSpec((tq, hv), lambda b, h, i: (b * nq + i, z_col0 + h))],
        out_specs=pl.BlockSpec((1, tq, hv), lambda b, h, i: (b, i, h)),
        out_shape=jax.ShapeDtypeStruct((batch, seq, MLA_HEADS * MLA_V), bf16),
        scratch_shapes=[pltpu.VMEM((n_chain, th, LANES), f32),
                        pltpu.VMEM((n_chain, th, LANES), f32),
                        pltpu.VMEM((n_chain, th, MLA_V), f32)],
        compiler_params=pltpu.CompilerParams(
            dimension_semantics=("parallel", "parallel", "arbitrary"),
            vmem_limit_bytes=VMEM_LIMIT),
        name="mla_flash",
    )(q, k, v, p)


def _shift_mix(s, prev_rows, mu):
    rolled = pltpu.roll(s, 1, 0)
    sub = 8
    first = lax.broadcasted_iota(jnp.int32, (sub, s.shape[1]), 0) == 0
    pieces = []
    for i, prev_row in enumerate(prev_rows):
        top = i * CHUNK
        pieces += [jnp.where(first, prev_row, rolled[top:top + sub]),
                   rolled[top + sub:top + CHUNK]]
    prev = jnp.concatenate(pieces, axis=0)
    return s + mu * (prev - s)


def _rwkv_kernel(r_ref, k_ref, v_ref, lora_ref, z_ref, mu_ref, mul_ref, w0_ref, a0_ref, wl_ref,
                 kk_ref, ka_ref, rk_ref, lnw_ref, lnb_ref, seg_ref, ltri_ref,
                 o_ref, state_ref, prev_ref, prevl_ref):
    c_len = CHUNK
    halves = range(PAIR)

    @pl.when(pl.program_id(1) == 0)
    def _():
        state_ref[...] = jnp.zeros(state_ref.shape, f32)
        prev_ref[...] = jnp.zeros(prev_ref.shape, f32)
        prevl_ref[...] = jnp.zeros(prevl_ref.shape, f32)

    def load(ref):
        x = ref[...].astype(f32)
        return x.reshape(PAIR * c_len, x.shape[-1])

    def last_rows(x):
        return [x[(i + 1) * c_len - 1:(i + 1) * c_len, :] for i in halves]

    streams = [load(ref) for ref in (r_ref, k_ref, v_ref)]
    lora_raw = load(lora_ref)
    r, k, v = (_shift_mix(x, [prev_ref[i, j:j + 1, :] for i in halves], mu_ref[j:j + 1, :])
               for j, x in enumerate(streams))
    lora = _shift_mix(lora_raw, [prevl_ref[i, 0:1, :] for i in halves], mul_ref[...])
    for j, x in enumerate(streams):
        for i, last in enumerate(last_rows(x)):
            prev_ref[i, j:j + 1, :] = last
    for i, last in enumerate(last_rows(lora_raw)):
        prevl_ref[i, 0:1, :] = last

    lane = lax.broadcasted_iota(jnp.int32, lora.shape, 1)
    lora = jnp.where(lane < LANES // 2, jnp.tanh(lora), lora)
    proj = jnp.dot(lora.astype(bf16), wl_ref[...], preferred_element_type=f32)
    w = -_softplus(-(w0_ref[...] + proj[:, :D])) - 0.5
    log_decay = -jnp.exp(w)
    a = _sigmoid(a0_ref[...] + proj[:, D:])
    kk = k * kk_ref[...]
    k = k * (1.0 + (a - 1.0) * ka_ref[...])

    cum = _chunk_cumsum(ltri_ref[...], log_decay)
    g_inc = jnp.exp(cum)
    g_inv = jnp.exp(-cum)
    g_exc = jnp.exp(cum - log_decay)
    g_end = [jnp.exp(last) for last in last_rows(cum)]

    seg = seg_ref[...]
    row = lax.broadcasted_iota(jnp.int32, (c_len, QUAD), 0)
    col = lax.broadcasted_iota(jnp.int32, (c_len, QUAD), 1)
    col_t = jnp.bitwise_and(col, c_len - 1)
    col_head = _div_pow2(col, RWKV_HEAD)
    strict = col_t < row
    incl = col_t <= row
    eye = (col_t == row).astype(f32)
    head_masks = [(col_head == g).astype(bf16) for g in range(QUAD // RWKV_HEAD)]
    block_diag = (_div_pow2(lax.broadcasted_iota(jnp.int32, (QUAD, QUAD), 0), RWKV_HEAD)
                  == _div_pow2(lax.broadcasted_iota(jnp.int32, (QUAD, QUAD), 1), RWKV_HEAD))

    def stack(y):
        y = y.astype(bf16)
        return jnp.concatenate([y * m for m in head_masks], axis=0)

    def bdot(x, y, dims=NN_DIMS):
        return lax.dot_general(x.astype(bf16), y.astype(bf16), dims, preferred_element_type=f32)

    def qprod(x, y):
        return bdot(x, stack(y))

    slabs = [slice(qd * QUAD, (qd + 1) * QUAD) for qd in range(N_QUADS)]
    items = [(i, qd) for i in halves for qd in range(N_QUADS)]
    each = lambda fn, *cols: [fn(*args) for args in zip(*cols)]
    cut = lambda x: [x[i * c_len:(i + 1) * c_len, slabs[qd]] for i, qd in items]

    kk_q = cut(kk)
    kk_sq = _seg_sums([x * x for x in kk_q], seg)
    kk_q = each(lambda x, sq: x * lax.rsqrt(jnp.maximum(sq, 1e-24)), kk_q, kk_sq)
    a_bar = each(lambda x, g: -x * g, kk_q, cut(g_exc))
    b_til = each(lambda x, a_q, g: x * a_q * g, kk_q, cut(a), cut(g_inv))
    k_til = each(jnp.multiply, cut(k), cut(g_inv))
    r_bar = each(jnp.multiply, cut(r), cut(g_inc))
    g_end_q = [g_end[i][:, slabs[qd]] for i, qd in items]
    b_end = each(jnp.multiply, b_til, g_end_q)
    k_end = each(jnp.multiply, k_til, g_end_q)
    v_q = cut(v)

    pair = each(lambda ab, rb, bt, kt: bdot(jnp.concatenate([ab, rb], axis=0),
                                            jnp.concatenate([stack(bt), stack(kt)], axis=0),
                                            NT_DIMS),
                a_bar, r_bar, b_til, k_til)
    a_ab = [jnp.where(strict, x[:c_len, :QUAD], 0.0) for x in pair]
    a_ak = [jnp.where(strict, x[:c_len, QUAD:], 0.0) for x in pair]
    a_rb = [jnp.where(incl, x[c_len:, :QUAD], 0.0) for x in pair]
    a_rk = [jnp.where(incl, x[c_len:, QUAD:], 0.0) for x in pair]

    t_inv = [eye + x for x in a_ab]
    power = each(qprod, a_ab, a_ab)
    span = 4
    while span < c_len:
        both = each(lambda t, pw: qprod(jnp.concatenate([t, pw], axis=0), pw), t_inv, power)
        t_inv = each(lambda t, x: t + x[:c_len], t_inv, both)
        power = [x[c_len:] for x in both]
        span *= 2
    t_inv = each(lambda t, pw: t + qprod(t, pw), t_inv, power)

    akv = each(lambda aak, ark, vq: qprod(jnp.concatenate([aak, ark], axis=0), vq),
               a_ak, a_rk, v_q)
    wu = each(lambda t, ab, x: bdot(t, jnp.concatenate([stack(ab), stack(x[:c_len])], axis=1)),
              t_inv, a_bar, akv)
    state = [state_ref[i, qd] for i, qd in items]
    on_state = each(lambda x, rb, st: bdot(jnp.concatenate([x[:, :QUAD], rb], axis=0), st,
                                           NT_DIMS), wu, r_bar, state)
    u = each(lambda hs, x: hs[:c_len] + x[:, QUAD:], on_state, wu)
    y = each(lambda hs, arb, u_q, x: hs[c_len:] + qprod(arb, u_q) + x[c_len:],
             on_state, a_rb, u, akv)
    upd = each(lambda u_q, vq, be, ke: bdot(jnp.concatenate([u_q, vq], axis=0),
                                            jnp.concatenate([be, ke], axis=0), TN_DIMS),
               u, v_q, b_end, k_end)
    for n, (i, qd) in enumerate(items):
        state_ref[i, qd] = state[n] * g_end_q[n] + jnp.where(block_diag, upd[n], 0.0)

    inv_n = 1.0 / RWKV_HEAD
    rkr = cut(r * k * rk_ref[...])
    sums = _seg_sums(y + rkr, seg)
    dev = each(lambda y_q, total: y_q - total * inv_n, y, sums[:len(items)])
    bonus = each(jnp.multiply, sums[len(items):], v_q)
    var = [total * inv_n for total in _seg_sums([d_q * d_q for d_q in dev], seg)]
    for n, (i, qd) in enumerate(items):
        sl = slabs[qd]
        y_q = dev[n] * lax.rsqrt(var[n] + RWKV_GN_EPS) * lnw_ref[:, sl] + lnb_ref[:, sl]
        o_ref[i, :, sl] = ((y_q + bonus[n]) * _silu(z_ref[i, :, sl].astype(f32))).astype(bf16)


def _rwkv(p, mu_rkv, mu_lora, w0, a0, w_lora, k_k, k_a, r_k, ln_w, ln_b, seg, ltri, batch, seq):
    nc = seq // CHUNK
    lora_col = T_SMALL * (D // LANES) + OFF_LORA // LANES
    tile = lambda t: pl.BlockSpec((PAIR, CHUNK, D), lambda b, c: (b, c, t))
    const = lambda shape: pl.BlockSpec(shape, lambda b, c: (0, 0))
    return pl.pallas_call(
        _rwkv_kernel,
        grid=(batch // PAIR, nc),
        in_specs=[tile(T_R), tile(T_K), tile(T_V),
                  pl.BlockSpec((PAIR, CHUNK, LANES), lambda b, c: (b, c, lora_col)),
                  tile(T_RZ),
                  const(mu_rkv.shape), const(mu_lora.shape), const((1, D)), const((1, D)),
                  const(w_lora.shape),
                  const((1, D)), const((1, D)), const((1, D)), const((1, D)), const((1, D)),
                  const(seg.shape), const(ltri.shape)],
        out_specs=pl.BlockSpec((PAIR, CHUNK, D), lambda b, c: (b, c, 0)),
        out_shape=jax.ShapeDtypeStruct((batch, seq, D), bf16),
        scratch_shapes=[pltpu.VMEM((PAIR, N_QUADS, QUAD, QUAD), f32),
                        pltpu.VMEM((PAIR, 8, D), f32), pltpu.VMEM((PAIR, 8, LANES), f32)],
        compiler_params=pltpu.CompilerParams(
            dimension_semantics=("parallel", "arbitrary"), vmem_limit_bytes=VMEM_LIMIT),
        name="rwkv7_chunk",
    )(p, p, p, p, p, mu_rkv, mu_lora, w0, a0, w_lora, k_k, k_a, r_k, ln_w, ln_b, seg, ltri)


def _gla_kernel(qk_ref, v_ref, gl_ref, z_ref, w2_ref, gb_ref, ng_ref, ltri_ref, o_ref, state_ref):
    c_len = CHUNK
    kw = GLA_HEADS * GLA_DK
    halves = range(PAIR)

    @pl.when(pl.program_id(1) == 0)
    def _():
        state_ref[...] = jnp.zeros(state_ref.shape, f32)

    qk = qk_ref[...].astype(f32).reshape(PAIR * c_len, 2 * kw)
    q = qk[:, :kw] * (GLA_DK ** -0.5)
    k = qk[:, kw:]
    gl = gl_ref[...].reshape(PAIR * c_len, LANES)
    gate = jnp.dot(gl, w2_ref[...], preferred_element_type=f32) + gb_ref[...]
    log_a = -_softplus(-gate) / GLA_GATE_NORM
    cum = _chunk_cumsum(ltri_ref[...], log_a)
    rows = [slice(i * c_len, (i + 1) * c_len) for i in halves]
    cum_last = [cum[sl.stop - 1:sl.stop, :] for sl in rows]
    q_dec = (q * jnp.exp(cum)).astype(bf16)
    k_inv = (k * jnp.exp(-cum)).astype(bf16)
    k_end = [(k[sl] * jnp.exp(last - cum[sl])).astype(bf16) for sl, last in zip(rows, cum_last)]
    decay = [jnp.exp(last) for last in cum_last]
    causal = (lax.broadcasted_iota(jnp.int32, (c_len, c_len), 1)
              <= lax.broadcasted_iota(jnp.int32, (c_len, c_len), 0))
    ks = [slice(h * GLA_DK, (h + 1) * GLA_DK) for h in range(GLA_HEADS)]
    vs = [slice(h * GLA_DV, (h + 1) * GLA_DV) for h in range(GLA_HEADS)]
    items = [(i, h) for i in halves for h in range(GLA_HEADS)]
    v_h = [v_ref[i, :, vs[h]] for i, h in items]
    state = [state_ref[i, h] for i, h in items]
    attn = [lax.dot_general(q_dec[rows[i], ks[h]], k_inv[rows[i], ks[h]], NT_DIMS,
                            preferred_element_type=f32) for i, h in items]
    attn = [jnp.where(causal, x, 0.0).astype(bf16) for x in attn]
    o = [jnp.dot(attn[n], v_h[n], preferred_element_type=f32)
         + lax.dot_general(q_dec[rows[i], ks[h]], state[n].astype(bf16), NT_DIMS,
                           preferred_element_type=f32) for n, (i, h) in enumerate(items)]
    for n, (i, h) in enumerate(items):
        state_ref[i, h] = state[n] * decay[i][:, ks[h]] + lax.dot_general(
            v_h[n], k_end[i][:, ks[h]], TN_DIMS, preferred_element_type=f32)
    for n, (i, h) in enumerate(items):
        o_h = o[n] * lax.rsqrt(jnp.mean(o[n] * o[n], axis=-1, keepdims=True) + NORM_EPS)
        gate_z = _silu(z_ref[i, :, vs[h]].astype(f32))
        o_ref[i, :, vs[h]] = (o_h * ng_ref[...] * gate_z).astype(bf16)


def _gla(p, w2_pad, g_bias, norm_g, ltri, batch, seq):
    nc = seq // CHUNK
    slab_col = T_SMALL * (D // LANES) + OFF_SLAB // LANES
    tile = lambda t: pl.BlockSpec((PAIR, CHUNK, D), lambda b, c: (b, c, t))
    const = lambda shape: pl.BlockSpec(shape, lambda b, c: (0, 0))
    return pl.pallas_call(
        _gla_kernel,
        grid=(batch // PAIR, nc),
        in_specs=[tile(T_GQK), tile(T_GV),
                  pl.BlockSpec((PAIR, CHUNK, LANES), lambda b, c: (b, c, slab_col)),
                  tile(T_GZ),
                  const(w2_pad.shape), const(g_bias.shape), const(norm_g.shape),
                  const(ltri.shape)],
        out_specs=pl.BlockSpec((PAIR, CHUNK, D), lambda b, c: (b, c, 0)),
        out_shape=jax.ShapeDtypeStruct((batch, seq, D), bf16),
        scratch_shapes=[pltpu.VMEM((PAIR, GLA_HEADS, GLA_DV, GLA_DK), f32)],
        compiler_params=pltpu.CompilerParams(
            dimension_semantics=("parallel", "arbitrary"), vmem_limit_bytes=VMEM_LIMIT),
        name="gla_chunk",
    )(p, p, p, p, w2_pad, g_bias, norm_g, ltri)


def _merge_kernel(om_ref, or_ref, og_ref, ga_ref, gb_ref, gc_ref, x_ref, gate_ref, np_ref,
                  wm_ref, wr_ref, wg_ref, wo_ref, o_ref):
    gate_of = lambda ref: _sigmoid(ref[...].astype(f32))
    merged = (gate_of(ga_ref) * jnp.dot(om_ref[...], wm_ref[...], preferred_element_type=f32)
              + gate_of(gb_ref) * jnp.dot(or_ref[...], wr_ref[...], preferred_element_type=f32)
              + gate_of(gc_ref) * jnp.dot(og_ref[...], wg_ref[...], preferred_element_type=f32))
    y = jnp.dot(merged.astype(bf16), wo_ref[...], preferred_element_type=f32)
    o_ref[...] = x_ref[...] + gate_ref[0] * _rms(y, np_ref[...])


def _merge(o_mla, o_rwkv, o_gla, p, x2d, gate, norm_post, w_mla, w_rwkv, w_gla, w_out, seq):
    tokens = x2d.shape[0]
    tm = min(512, seq)
    per_b = seq // tm
    row = lambda i: (i, 0)
    const = lambda i: (0, 0)
    return pl.pallas_call(
        _merge_kernel,
        grid=(tokens // tm,),
        in_specs=[pl.BlockSpec((tm, D), row), pl.BlockSpec((tm, D), row),
                  pl.BlockSpec((tm, D), row),
                  pl.BlockSpec((tm, D), lambda i: (i, T_GA)),
                  pl.BlockSpec((tm, D), lambda i: (i, T_GB)),
                  pl.BlockSpec((tm, D), lambda i: (i, T_GC)),
                  pl.BlockSpec((tm, D), row),
                  pl.BlockSpec((1, 1, D), lambda i: (i // per_b, 0, 0)),
                  pl.BlockSpec((1, D), const),
                  pl.BlockSpec((D, D), const), pl.BlockSpec((D, D), const),
                  pl.BlockSpec((D, D), const), pl.BlockSpec((D, D), const)],
        out_specs=pl.BlockSpec((tm, D), row),
        out_shape=jax.ShapeDtypeStruct((tokens, D), f32),
        compiler_params=pltpu.CompilerParams(
            dimension_semantics=("parallel",), vmem_limit_bytes=VMEM_LIMIT),
        name="merge_outproj",
    )(o_mla, o_rwkv, o_gla, p, p, p, x2d, gate, norm_post, w_mla, w_rwkv, w_gla, w_out)


def _rot_half_cols(w):
    half = w.shape[-1] // 2
    return jnp.concatenate([-w[..., half:], w[..., :half]], axis=-1)


def _permute_w_in(w):
    sizes = (512, 256, 64, 1024, 1024, 1024, 1024, 64, 64, 1024, 512, 512, 1024, 16, 1024,
             1024, 1024, 1024)
    parts = []
    start = 0
    for size in sizes:
        parts.append(w[:, start:start + size])
        start += size
    (c_q, c_kv, k_rope, mla_z, r, k, v, w_l, a_l, r_z, g_q, g_k, g_v, g_l, g_z,
     gate_a, gate_b, gate_c) = parts
    pad = jnp.zeros((w.shape[0], OFF_LORA - OFF_SLAB - MLA_ROPE - g_l.shape[1]), w.dtype)
    return jnp.concatenate([c_q, c_kv, k_rope, g_l, pad, w_l, a_l, mla_z, r, k, v, r_z,
                            g_q, g_k, g_v, g_z, gate_a, gate_b, gate_c], axis=1).astype(bf16)


def _mla_weights(w_uq, w_ukv):
    n_in = w_uq.shape[0]
    wq = w_uq.reshape(n_in, MLA_HEADS, MLA_NOPE + MLA_ROPE)
    wq = jnp.concatenate([wq, _rot_half_cols(wq[..., MLA_NOPE:])], axis=-1)
    wkv = w_ukv.reshape(w_ukv.shape[0], MLA_HEADS, MLA_NOPE + MLA_V)
    return wq.transpose(1, 0, 2).astype(bf16), wkv.transpose(1, 0, 2).astype(bf16)


def kernel(x, c, positions, ada_w, ada_b, norm_pre, norm_post, w_in, rwkv_mu, mla_q_norm,
           mla_kv_norm, mla_w_uq, mla_w_ukv, mla_w_o, rwkv_w0, rwkv_w2, rwkv_a0, rwkv_a2,
           rwkv_k_k, rwkv_k_a, rwkv_r_k, rwkv_ln_w, rwkv_ln_b, rwkv_w_o, gla_w2, gla_b,
           gla_norm, gla_w_o, w_out):
    batch, seq, _ = x.shape
    assert batch % PAIR == 0 and seq % (2 * CHUNK) == 0, (batch, seq)
    depth = w_in.shape[0]
    tokens = batch * seq
    x2d = x.reshape(tokens, D)
    pos2d = positions.reshape(tokens, 1)
    half = MLA_ROPE // 2
    inv_freq = ROPE_BASE ** (-jnp.arange(0, MLA_ROPE, 2, dtype=f32) / MLA_ROPE)
    invf = jnp.tile(inv_freq, LANES // half).reshape(1, LANES)
    idx = jnp.arange(QUAD)
    seg = (idx[:, None] // RWKV_HEAD == idx[None, :] // RWKV_HEAD).astype(bf16)
    tri = jnp.arange(PAIR * CHUNK)
    ltri = ((tri[None, :] <= tri[:, None])
            & (tri[None, :] // CHUNK == tri[:, None] // CHUNK)).astype(bf16)
    row1 = lambda t: t.reshape(1, -1)

    mod = _modulation(c, ada_w, ada_b)
    for l in range(depth):
        shift = mod[l, :, :D].reshape(batch, 1, D)
        scale = mod[l, :, D:2 * D].reshape(batch, 1, D)
        gate = mod[l, :, 2 * D:].reshape(batch, 1, D)
        p = _inproj(x2d, scale, shift, row1(norm_pre[l]), _permute_w_in(w_in[l]), seq)

        wq, wkv = _mla_weights(mla_w_uq[l], mla_w_ukv[l])
        q, k, v = _mla_prep(p, pos2d, invf, row1(mla_q_norm[l]), row1(mla_kv_norm[l]), wq, wkv,
                            batch, seq)
        o_mla = _flash(q, k, v, p, batch, seq).reshape(tokens, D)

        mu = rwkv_mu[l]
        mu_rkv = jnp.zeros((8, D), f32).at[:3].set(mu[:3 * D].reshape(3, D))
        w_lora = jnp.zeros((LANES, 2 * D), f32)
        w_lora = w_lora.at[:LANES // 2, :D].set(rwkv_w2[l]).at[LANES // 2:, D:].set(rwkv_a2[l])
        p3 = p.reshape(batch, seq, P_COLS)
        o_rwkv = _rwkv(p3, mu_rkv, row1(mu[3 * D:]), row1(rwkv_w0[l]), row1(rwkv_a0[l]),
                       w_lora.astype(bf16), row1(rwkv_k_k[l]), row1(rwkv_k_a[l]),
                       row1(rwkv_r_k[l]), row1(rwkv_ln_w[l]), row1(rwkv_ln_b[l]), seg, ltri,
                       batch, seq).reshape(tokens, D)

        w2_pad = jnp.zeros((LANES, GLA_HEADS * GLA_DK), f32)
        w2_pad = w2_pad.at[OFF_GL:OFF_GL + gla_w2.shape[1]].set(gla_w2[l]).astype(bf16)
        o_gla = _gla(p3, w2_pad, row1(gla_b[l]), row1(gla_norm[l]), ltri, batch,
                     seq).reshape(tokens, D)

        x2d = _merge(o_mla, o_rwkv, o_gla, p, x2d, gate, row1(norm_post[l]),
                     mla_w_o[l].astype(bf16), rwkv_w_o[l].astype(bf16),
                     gla_w_o[l].astype(bf16), w_out[l].astype(bf16), seq)
    return x2d.reshape(batch, seq, D)
```

```python
import functools

import jax
import jax.numpy as jnp
from jax import lax
from jax.experimental import pallas as pl
from jax.experimental.pallas import tpu as pltpu

f32 = jnp.float32
bf16 = jnp.bfloat16

D = 1024
CHUNK = 64
PAIR = 2
NORM_EPS = 1e-6
MLA_HEADS = 8
MLA_NOPE = 128
MLA_ROPE = 64
MLA_V = 128
MLA_QK_PAD = 256
FLASH_HEADS = 2
ROPE_BASE = 10000.0
LOG2_E = 1.4426950408889634
RWKV_HEAD = 64
RWKV_GN_EPS = 64e-5
GLA_HEADS = 4
GLA_DK = 128
GLA_DV = 256
GLA_GATE_NORM = 16.0
LANES = 128
SUB = 8
QUAD = 128
N_QUADS = D // QUAD
P_COLS = 12 * D
VMEM_LIMIT = 56 * 1024 * 1024

T_SMALL, T_MLA_Z, T_R, T_K, T_V, T_RZ, T_GQK, T_GV, T_GZ, T_GA, T_GB, T_GC = range(12)
OFF_CKV = 512
OFF_SLAB = 768
OFF_GL = 64
OFF_LORA = 896

NN_DIMS = (((1,), (0,)), ((), ()))
NT_DIMS = (((1,), (1,)), ((), ()))
TN_DIMS = (((0,), (0,)), ((), ()))


def _div_pow2(x, n):
    return lax.shift_right_logical(x, jnp.int32(n.bit_length() - 1))


def _softplus(x):
    return jnp.maximum(x, 0.0) + jnp.log(1.0 + jnp.exp(-jnp.abs(x)))


def _sigmoid(x):
    return 1.0 / (1.0 + jnp.exp(-x))


def _silu(x):
    return x * _sigmoid(x)


def _split_bf16(x):
    hi = x.astype(bf16)
    return hi, (x - hi.astype(f32)).astype(bf16)


def _seg_sums(xs, seg):
    rows = xs[0].shape[0]
    out = jnp.dot(jnp.concatenate(xs, axis=0).astype(bf16), seg, preferred_element_type=f32)
    return [out[i * rows:(i + 1) * rows] for i in range(len(xs))]


def _chunk_cumsum(ltri, x):
    hi, lo = _split_bf16(x)
    return (jnp.dot(ltri, hi, preferred_element_type=f32)
            + jnp.dot(ltri, lo, preferred_element_type=f32))


def _rms(x, gain):
    return x * lax.rsqrt(jnp.mean(x * x, axis=-1, keepdims=True) + NORM_EPS) * gain


def _mod_kernel(c_ref, w_ref, b_ref, o_ref):
    c = _silu(c_ref[...]).astype(bf16)
    o_ref[0] = jnp.dot(c, w_ref[0].astype(bf16), preferred_element_type=f32) + b_ref[0]


def _modulation(c, ada_w, ada_b):
    depth, _, n3 = ada_w.shape
    batch = c.shape[0]
    return pl.pallas_call(
        _mod_kernel,
        grid=(depth, n3 // D),
        in_specs=[pl.BlockSpec((batch, D), lambda l, j: (0, 0)),
                  pl.BlockSpec((1, D, D), lambda l, j: (l, 0, j)),
                  pl.BlockSpec((1, 1, D), lambda l, j: (l, 0, j))],
        out_specs=pl.BlockSpec((1, batch, D), lambda l, j: (l, 0, j)),
        out_shape=jax.ShapeDtypeStruct((depth, batch, n3), f32),
        name="adaln_mod",
    )(c, ada_w, ada_b.reshape(depth, 1, n3))


def _inproj_kernel(x_ref, scale_ref, shift_ref, g_ref, w_ref, o_ref, h_ref):
    @pl.when(pl.program_id(1) == 0)
    def _():
        h = _rms(x_ref[...], g_ref[...]) * (1.0 + scale_ref[0]) + shift_ref[0]
        h_ref[...] = h.astype(bf16)

    o_ref[...] = jnp.dot(h_ref[...], w_ref[...], preferred_element_type=f32).astype(bf16)


def _inproj(x2d, scale, shift, gain, w, seq):
    tokens = x2d.shape[0]
    tm = min(2048, seq)
    tn = D
    per_b = seq // tm
    return pl.pallas_call(
        _inproj_kernel,
        grid=(tokens // tm, P_COLS // tn),
        in_specs=[pl.BlockSpec((tm, D), lambda i, j: (i, 0)),
                  pl.BlockSpec((1, 1, D), lambda i, j: (i // per_b, 0, 0)),
                  pl.BlockSpec((1, 1, D), lambda i, j: (i // per_b, 0, 0)),
                  pl.BlockSpec((1, D), lambda i, j: (0, 0)),
                  pl.BlockSpec((D, tn), lambda i, j: (0, j))],
        out_specs=pl.BlockSpec((tm, tn), lambda i, j: (i, j)),
        out_shape=jax.ShapeDtypeStruct((tokens, P_COLS), bf16),
        scratch_shapes=[pltpu.VMEM((tm, D), bf16)],
        compiler_params=pltpu.CompilerParams(
            dimension_semantics=("parallel", "arbitrary"), vmem_limit_bytes=VMEM_LIMIT),
        name="norm_inproj",
    )(x2d, scale, shift, gain, w)


def _rope_kernel(pos_ref, invf_ref, cos_ref, sin_ref):
    ang = pos_ref[...].astype(f32) * invf_ref[...]
    cos_ref[...] = jnp.cos(ang)
    sin_ref[...] = jnp.sin(ang)


def _rope_table(positions, tokens):
    half = MLA_ROPE // 2
    pack = LANES // half
    rows = tokens // pack
    tr = min(2048, rows)
    inv_freq = ROPE_BASE ** (-jnp.arange(0, MLA_ROPE, 2, dtype=f32) / MLA_ROPE)
    invf = jnp.tile(inv_freq, pack).reshape(1, LANES)
    pos = jnp.repeat(positions.reshape(rows, pack), half, axis=1)
    cos, sin = pl.pallas_call(
        _rope_kernel,
        grid=(rows // tr,),
        in_specs=[pl.BlockSpec((tr, LANES), lambda i: (i, 0)),
                  pl.BlockSpec((1, LANES), lambda i: (0, 0))],
        out_specs=[pl.BlockSpec((tr, LANES), lambda i: (i, 0))] * 2,
        out_shape=[jax.ShapeDtypeStruct((rows, LANES), f32)] * 2,
        name="rope_table",
    )(pos, invf)
    cos, sin = cos.reshape(tokens, half), sin.reshape(tokens, half)
    return jnp.concatenate([cos, cos, sin, sin], axis=1)


def _mla_prep_kernel(p_ref, rope_ref, qn_ref, kvn_ref, wq_ref, wkv_ref, q_ref, k_ref, v_ref):
    p = p_ref[...].astype(f32)
    tm = p.shape[0]
    tv = v_ref.shape[-1]
    cq = _rms(p[:, :OFF_CKV], qn_ref[...]).astype(bf16)
    ckv = _rms(p[:, OFF_CKV:OFF_SLAB], kvn_ref[...]).astype(bf16)
    slab = p[:, OFF_SLAB:OFF_SLAB + LANES]
    cos_sin = rope_ref[...]
    lane = lax.broadcasted_iota(jnp.int32, (tm, LANES), 1)
    half = MLA_ROPE // 2
    rot = jnp.where(lane < half, -pltpu.roll(slab, LANES - half, 1), pltpu.roll(slab, half, 1))
    in_rope = lane < MLA_ROPE
    k_rope = slab * cos_sin + rot * pltpu.roll(cos_sin, MLA_ROPE, 1)
    k_rope = jnp.where(in_rope, k_rope, 0.0).astype(bf16)
    scale = (MLA_NOPE + MLA_ROPE) ** -0.5 * LOG2_E
    for h in range(MLA_HEADS):
        rq = jnp.dot(cq, wq_ref[h], preferred_element_type=f32)
        a = rq[:, MLA_NOPE:] * cos_sin
        q_rope = a + pltpu.roll(a, MLA_ROPE, 1)
        q_ref[0, h, :, :MLA_NOPE] = (rq[:, :MLA_NOPE] * scale).astype(bf16)
        q_ref[0, h, :, MLA_NOPE:] = jnp.where(in_rope, q_rope * scale, 0.0).astype(bf16)
        rkv = jnp.dot(ckv, wkv_ref[h], preferred_element_type=f32)
        k_ref[0, h, :, :MLA_NOPE] = rkv[:, :MLA_NOPE].astype(bf16)
        k_ref[0, h, :, MLA_NOPE:] = k_rope
        v_t = rkv[:, MLA_NOPE:].T
        for j in range(v_ref.shape[2]):
            v_ref[0, h, j] = v_t[:, j * tv:(j + 1) * tv].astype(bf16)


def _mla_prep(p, rope, q_norm, kv_norm, wq, wkv, batch, seq):
    tm = min(512, seq)
    tv = _flash_tile(seq)
    per_b = seq // tm
    dqk = MLA_QK_PAD
    const2 = lambda b, i: (0, 0)
    const3 = lambda b, i: (0, 0, 0)
    return pl.pallas_call(
        _mla_prep_kernel,
        grid=(batch, per_b),
        in_specs=[pl.BlockSpec((tm, D), lambda b, i: (b * per_b + i, T_SMALL)),
                  pl.BlockSpec((tm, LANES), lambda b, i: (b * per_b + i, 0)),
                  pl.BlockSpec((1, OFF_CKV), const2),
                  pl.BlockSpec((1, OFF_SLAB - OFF_CKV), const2),
                  pl.BlockSpec(wq.shape, const3),
                  pl.BlockSpec(wkv.shape, const3)],
        out_specs=[pl.BlockSpec((1, MLA_HEADS, tm, dqk), lambda b, i: (b, 0, i, 0)),
                   pl.BlockSpec((1, MLA_HEADS, tm, dqk), lambda b, i: (b, 0, i, 0)),
                   pl.BlockSpec((1, MLA_HEADS, tm // tv, MLA_V, tv),
                                lambda b, i: (b, 0, i, 0, 0))],
        out_shape=[jax.ShapeDtypeStruct((batch, MLA_HEADS, seq, dqk), bf16),
                   jax.ShapeDtypeStruct((batch, MLA_HEADS, seq, dqk), bf16),
                   jax.ShapeDtypeStruct((batch, MLA_HEADS, seq // tv, MLA_V, tv), bf16)],
        compiler_params=pltpu.CompilerParams(
            dimension_semantics=("parallel", "parallel"), vmem_limit_bytes=VMEM_LIMIT),
        name="mla_prep",
    )(p, rope, q_norm, kv_norm, wq, wkv)


def _flash_tile(seq):
    return min(512, seq // 2)


def _flash_kernel(q_ref, k_ref, v_ref, z_ref, o_ref, m_ref, l_ref, acc_ref, *, th):
    qi = pl.program_id(2)
    m_ref[...] = jnp.full(m_ref.shape, -jnp.inf, f32)
    l_ref[...] = jnp.zeros(l_ref.shape, f32)
    acc_ref[...] = jnp.zeros(acc_ref.shape, f32)
    groups = th // SUB

    def run(chains):
        slot = [2 * h + half for h, half, _, _ in chains]
        s = [lax.dot_general(k_ref[0, h, pl.ds(pl.multiple_of(ki * th, th), th), :],
                             q_ref[0, h, half * th:(half + 1) * th, :], NT_DIMS,
                             preferred_element_type=f32)
             for h, half, ki, _ in chains]
        if any(diag for _, _, _, diag in chains):
            k_chunk = _div_pow2(lax.broadcasted_iota(jnp.int32, (th, th), 0), CHUNK)
            q_chunk = _div_pow2(lax.broadcasted_iota(jnp.int32, (th, th), 1), CHUNK)
            allowed = k_chunk <= q_chunk
            s = [jnp.where(allowed, x, -jnp.inf) if diag else x
                 for x, (_, _, _, diag) in zip(s, chains)]
        s = [x.reshape(groups, SUB, th) for x in s]
        m_prev = [m_ref[i] for i in slot]
        m_new = [jnp.maximum(mp, jnp.max(jnp.max(x, axis=0), axis=0, keepdims=True))
                 for mp, x in zip(m_prev, s)]
        alpha = [jnp.exp2(mp - mn) for mp, mn in zip(m_prev, m_new)]
        prob = [jnp.exp2(x - mn[None]) for x, mn in zip(s, m_new)]
        for i, al, pr in zip(slot, alpha, prob):
            l_ref[i] = al * l_ref[i] + jnp.sum(pr, axis=0)
        pv = [jnp.dot(v_ref[0, h, ki], pr.reshape(th, th).astype(bf16),
                      preferred_element_type=f32)
              for pr, (h, _, ki, _) in zip(prob, chains)]
        for i, al, x, mn in zip(slot, alpha, pv, m_new):
            acc = acc_ref[i].reshape(MLA_V // SUB, SUB, th) * al[None]
            acc_ref[i] = acc.reshape(MLA_V, th) + x
            m_ref[i] = mn

    heads = range(FLASH_HEADS)
    n_full = 2 * qi

    def body(pair_idx, carry):
        for ki in (2 * pair_idx, 2 * pair_idx + 1):
            run([(h, half, ki, False) for h in heads for half in range(2)])
        return carry

    lax.fori_loop(0, qi, body, 0)
    run([(h, half, n_full, half == 0) for h in heads for half in range(2)])
    run([(h, 1, n_full + 1, True) for h in heads])
    for h in heads:
        lanes = slice(h * MLA_V, (h + 1) * MLA_V)
        for half in range(2):
            rows = slice(half * th, (half + 1) * th)
            i = 2 * h + half
            inv_l = 1.0 / jnp.sum(l_ref[i], axis=0, keepdims=True)
            o = (acc_ref[i] * inv_l).T
            o_ref[0, rows, lanes] = (o * _silu(z_ref[rows, lanes].astype(f32))).astype(bf16)


def _flash(q, k, v_t, p, batch, seq):
    th = _flash_tile(seq)
    tq = 2 * th
    nq = seq // tq
    dqk = MLA_QK_PAD
    hv = FLASH_HEADS * MLA_V
    z_col0 = T_MLA_Z * (D // hv)
    n_chain = 2 * FLASH_HEADS
    return pl.pallas_call(
        functools.partial(_flash_kernel, th=th),
        grid=(batch, MLA_HEADS // FLASH_HEADS, nq),
        in_specs=[pl.BlockSpec((1, FLASH_HEADS, tq, dqk), lambda b, h, i: (b, h, i, 0)),
                  pl.BlockSpec((1, FLASH_HEADS, seq, dqk), lambda b, h, i: (b, h, 0, 0)),
                  pl.BlockSpec((1, FLASH_HEADS, seq // th, MLA_V, th),
                               lambda b, h, i: (b, h, 0, 0, 0)),
                  pl.BlockSpec((tq, hv), lambda b, h, i: (b * nq + i, z_col0 + h))],
        out_specs=pl.BlockSpec((1, tq, hv), lambda b, h, i: (b, i, h)),
        out_shape=jax.ShapeDtypeStruct((batch, seq, MLA_HEADS * MLA_V), bf16),
        scratch_shapes=[pltpu.VMEM((n_chain, SUB, th), f32),
                        pltpu.VMEM((n_chain, SUB, th), f32),
                        pltpu.VMEM((n_chain, MLA_V, th), f32)],
        compiler_params=pltpu.CompilerParams(
            dimension_semantics=("parallel", "parallel", "arbitrary"),
            vmem_limit_bytes=VMEM_LIMIT),
        name="mla_flash",
    )(q, k, v_t, p)


def _shift_mix(s, prev_rows, mu):
    rolled = pltpu.roll(s, 1, 0)
    sub = SUB
    first = lax.broadcasted_iota(jnp.int32, (sub, s.shape[1]), 0) == 0
    pieces = []
    for i, prev_row in enumerate(prev_rows):
        top = i * CHUNK
        pieces += [jnp.where(first, prev_row, rolled[top:top + sub]),
                   rolled[top + sub:top + CHUNK]]
    prev = jnp.concatenate(pieces, axis=0)
    return s + mu * (prev - s)


def _rwkv_kernel(r_ref, k_ref, v_ref, lora_ref, z_ref, mu_ref, mul_ref, w0_ref, a0_ref, wl_ref,
                 kk_ref, ka_ref, rk_ref, lnw_ref, lnb_ref, seg_ref, ltri_ref,
                 o_ref, state_ref, prev_ref, prevl_ref):
    c_len = CHUNK
    halves = range(PAIR)

    @pl.when(pl.program_id(1) == 0)
    def _():
        state_ref[...] = jnp.zeros(state_ref.shape, f32)
        prev_ref[...] = jnp.zeros(prev_ref.shape, f32)
        prevl_ref[...] = jnp.zeros(prevl_ref.shape, f32)

    def load(ref):
        x = ref[...].astype(f32)
        return x.reshape(PAIR * c_len, x.shape[-1])

    def last_rows(x):
        return [x[(i + 1) * c_len - 1:(i + 1) * c_len, :] for i in halves]

    streams = [load(ref) for ref in (r_ref, k_ref, v_ref)]
    lora_raw = load(lora_ref)
    r, k, v = (_shift_mix(x, [prev_ref[i, j:j + 1, :] for i in halves], mu_ref[j:j + 1, :])
               for j, x in enumerate(streams))
    lora = _shift_mix(lora_raw, [prevl_ref[i, 0:1, :] for i in halves], mul_ref[...])
    for j, x in enumerate(streams):
        for i, last in enumerate(last_rows(x)):
            prev_ref[i, j:j + 1, :] = last
    for i, last in enumerate(last_rows(lora_raw)):
        prevl_ref[i, 0:1, :] = last

    lane = lax.broadcasted_iota(jnp.int32, lora.shape, 1)
    lora = jnp.where(lane < LANES // 2, jnp.tanh(lora), lora)
    proj = jnp.dot(lora.astype(bf16), wl_ref[...], preferred_element_type=f32)
    w = -_softplus(-(w0_ref[...] + proj[:, :D])) - 0.5
    log_decay = -jnp.exp(w)
    a = _sigmoid(a0_ref[...] + proj[:, D:])
    kk = k * kk_ref[...]
    k = k * (1.0 + (a - 1.0) * ka_ref[...])

    cum = _chunk_cumsum(ltri_ref[...], log_decay)
    g_inc = jnp.exp(cum)
    g_inv = jnp.exp(-cum)
    g_exc = jnp.exp(cum - log_decay)
    g_end = [jnp.exp(last) for last in last_rows(cum)]

    seg = seg_ref[...]
    row = lax.broadcasted_iota(jnp.int32, (c_len, QUAD), 0)
    col = lax.broadcasted_iota(jnp.int32, (c_len, QUAD), 1)
    col_t = jnp.bitwise_and(col, c_len - 1)
    col_head = _div_pow2(col, RWKV_HEAD)
    strict = col_t < row
    incl = col_t <= row
    eye = (col_t == row).astype(f32)
    head_masks = [(col_head == g).astype(bf16) for g in range(QUAD // RWKV_HEAD)]
    block_diag = (_div_pow2(lax.broadcasted_iota(jnp.int32, (QUAD, QUAD), 0), RWKV_HEAD)
                  == _div_pow2(lax.broadcasted_iota(jnp.int32, (QUAD, QUAD), 1), RWKV_HEAD))

    def stack(y):
        y = y.astype(bf16)
        return jnp.concatenate([y * m for m in head_masks], axis=0)

    def bdot(x, y, dims=NN_DIMS):
        return lax.dot_general(x.astype(bf16), y.astype(bf16), dims, preferred_element_type=f32)

    def qprod(x, y):
        return bdot(x, stack(y))

    slabs = [slice(qd * QUAD, (qd + 1) * QUAD) for qd in range(N_QUADS)]
    items = [(i, qd) for i in halves for qd in range(N_QUADS)]
    each = lambda fn, *cols: [fn(*args) for args in zip(*cols)]
    cut = lambda x: [x[i * c_len:(i + 1) * c_len, slabs[qd]] for i, qd in items]

    kk_q = cut(kk)
    kk_sq = _seg_sums([x * x for x in kk_q], seg)
    kk_q = each(lambda x, sq: x * lax.rsqrt(jnp.maximum(sq, 1e-24)), kk_q, kk_sq)
    a_bar = each(lambda x, g: -x * g, kk_q, cut(g_exc))
    b_til = each(lambda x, a_q, g: x * a_q * g, kk_q, cut(a), cut(g_inv))
    k_til = each(jnp.multiply, cut(k), cut(g_inv))
    r_bar = each(jnp.multiply, cut(r), cut(g_inc))
    g_end_q = [g_end[i][:, slabs[qd]] for i, qd in items]
    b_end = each(jnp.multiply, b_til, g_end_q)
    k_end = each(jnp.multiply, k_til, g_end_q)
    v_q = cut(v)

    pair = each(lambda ab, rb, bt, kt: bdot(jnp.concatenate([ab, rb], axis=0),
                                            jnp.concatenate([stack(bt), stack(kt)], axis=0),
                                            NT_DIMS),
                a_bar, r_bar, b_til, k_til)
    a_ab = [jnp.where(strict, x[:c_len, :QUAD], 0.0) for x in pair]
    a_ak = [jnp.where(strict, x[:c_len, QUAD:], 0.0) for x in pair]
    a_rb = [jnp.where(incl, x[c_len:, :QUAD], 0.0) for x in pair]
    a_rk = [jnp.where(incl, x[c_len:, QUAD:], 0.0) for x in pair]

    t_inv = [eye + x for x in a_ab]
    power = each(qprod, a_ab, a_ab)
    span = 4
    while span < c_len:
        both = each(lambda t, pw: qprod(jnp.concatenate([t, pw], axis=0), pw), t_inv, power)
        t_inv = each(lambda t, x: t + x[:c_len], t_inv, both)
        power = [x[c_len:] for x in both]
        span *= 2
    t_inv = each(lambda t, pw: t + qprod(t, pw), t_inv, power)

    akv = each(lambda aak, ark, vq: qprod(jnp.concatenate([aak, ark], axis=0), vq),
               a_ak, a_rk, v_q)
    wu = each(lambda t, ab, x: bdot(t, jnp.concatenate([stack(ab), stack(x[:c_len])], axis=1)),
              t_inv, a_bar, akv)
    state = [state_ref[i, qd] for i, qd in items]
    on_state = each(lambda x, rb, st: bdot(jnp.concatenate([x[:, :QUAD], rb], axis=0), st,
                                           NT_DIMS), wu, r_bar, state)
    u = each(lambda hs, x: hs[:c_len] + x[:, QUAD:], on_state, wu)
    y = each(lambda hs, arb, u_q, x: hs[c_len:] + qprod(arb, u_q) + x[c_len:],
             on_state, a_rb, u, akv)
    upd = each(lambda u_q, vq, be, ke: bdot(jnp.concatenate([u_q, vq], axis=0),
                                            jnp.concatenate([be, ke], axis=0), TN_DIMS),
               u, v_q, b_end, k_end)
    for n, (i, qd) in enumerate(items):
        state_ref[i, qd] = state[n] * g_end_q[n] + jnp.where(block_diag, upd[n], 0.0)

    inv_n = 1.0 / RWKV_HEAD
    rkr = cut(r * k * rk_ref[...])
    sums = _seg_sums(y + rkr, seg)
    dev = each(lambda y_q, total: y_q - total * inv_n, y, sums[:len(items)])
    bonus = each(jnp.multiply, sums[len(items):], v_q)
    var = [total * inv_n for total in _seg_sums([d_q * d_q for d_q in dev], seg)]
    for n, (i, qd) in enumerate(items):
        sl = slabs[qd]
        y_q = dev[n] * lax.rsqrt(var[n] + RWKV_GN_EPS) * lnw_ref[:, sl] + lnb_ref[:, sl]
        o_ref[i, :, sl] = ((y_q + bonus[n]) * _silu(z_ref[i, :, sl].astype(f32))).astype(bf16)


def _rwkv(p, mu_rkv, mu_lora, w0, a0, w_lora, k_k, k_a, r_k, ln_w, ln_b, seg, ltri, batch, seq):
    nc = seq // CHUNK
    lora_col = T_SMALL * (D // LANES) + OFF_LORA // LANES
    tile = lambda t: pl.BlockSpec((PAIR, CHUNK, D), lambda b, c: (b, c, t))
    const = lambda shape: pl.BlockSpec(shape, lambda b, c: (0, 0))
    return pl.pallas_call(
        _rwkv_kernel,
        grid=(batch // PAIR, nc),
        in_specs=[tile(T_R), tile(T_K), tile(T_V),
                  pl.BlockSpec((PAIR, CHUNK, LANES), lambda b, c: (b, c, lora_col)),
                  tile(T_RZ),
                  const(mu_rkv.shape), const(mu_lora.shape), const((1, D)), const((1, D)),
                  const(w_lora.shape),
                  const((1, D)), const((1, D)), const((1, D)), const((1, D)), const((1, D)),
                  const(seg.shape), const(ltri.shape)],
        out_specs=pl.BlockSpec((PAIR, CHUNK, D), lambda b, c: (b, c, 0)),
        out_shape=jax.ShapeDtypeStruct((batch, seq, D), bf16),
        scratch_shapes=[pltpu.VMEM((PAIR, N_QUADS, QUAD, QUAD), f32),
                        pltpu.VMEM((PAIR, 8, D), f32), pltpu.VMEM((PAIR, 8, LANES), f32)],
        compiler_params=pltpu.CompilerParams(
            dimension_semantics=("parallel", "arbitrary"), vmem_limit_bytes=VMEM_LIMIT),
        name="rwkv7_chunk",
    )(p, p, p, p, p, mu_rkv, mu_lora, w0, a0, w_lora, k_k, k_a, r_k, ln_w, ln_b, seg, ltri)


def _gla_kernel(qk_ref, v_ref, gl_ref, z_ref, w2_ref, gb_ref, ng_ref, ltri_ref, o_ref, state_ref):
    c_len = CHUNK
    kw = GLA_HEADS * GLA_DK
    halves = range(PAIR)

    @pl.when(pl.program_id(1) == 0)
    def _():
        state_ref[...] = jnp.zeros(state_ref.shape, f32)

    qk = qk_ref[...].astype(f32).reshape(PAIR * c_len, 2 * kw)
    q = qk[:, :kw] * (GLA_DK ** -0.5)
    k = qk[:, kw:]
    gl = gl_ref[...].reshape(PAIR * c_len, LANES)
    gate = jnp.dot(gl, w2_ref[...], preferred_element_type=f32) + gb_ref[...]
    log_a = -_softplus(-gate) / GLA_GATE_NORM
    cum = _chunk_cumsum(ltri_ref[...], log_a)
    rows = [slice(i * c_len, (i + 1) * c_len) for i in halves]
    cum_last = [cum[sl.stop - 1:sl.stop, :] for sl in rows]
    q_dec = (q * jnp.exp(cum)).astype(bf16)
    k_inv = (k * jnp.exp(-cum)).astype(bf16)
    k_end = [(k[sl] * jnp.exp(last - cum[sl])).astype(bf16) for sl, last in zip(rows, cum_last)]
    decay = [jnp.exp(last) for last in cum_last]
    causal = (lax.broadcasted_iota(jnp.int32, (c_len, c_len), 1)
              <= lax.broadcasted_iota(jnp.int32, (c_len, c_len), 0))
    ks = [slice(h * GLA_DK, (h + 1) * GLA_DK) for h in range(GLA_HEADS)]
    vs = [slice(h * GLA_DV, (h + 1) * GLA_DV) for h in range(GLA_HEADS)]
    items = [(i, h) for i in halves for h in range(GLA_HEADS)]
    v_h = [v_ref[i, :, vs[h]] for i, h in items]
    state = [state_ref[i, h] for i, h in items]
    attn = [lax.dot_general(q_dec[rows[i], ks[h]], k_inv[rows[i], ks[h]], NT_DIMS,
                            preferred_element_type=f32) for i, h in items]
    attn = [jnp.where(causal, x, 0.0).astype(bf16) for x in attn]
    o = [jnp.dot(attn[n], v_h[n], preferred_element_type=f32)
         + lax.dot_general(q_dec[rows[i], ks[h]], state[n].astype(bf16), NT_DIMS,
                           preferred_element_type=f32) for n, (i, h) in enumerate(items)]
    for n, (i, h) in enumerate(items):
        state_ref[i, h] = state[n] * decay[i][:, ks[h]] + lax.dot_general(
            v_h[n], k_end[i][:, ks[h]], TN_DIMS, preferred_element_type=f32)
    for n, (i, h) in enumerate(items):
        o_h = o[n] * lax.rsqrt(jnp.mean(o[n] * o[n], axis=-1, keepdims=True) + NORM_EPS)
        gate_z = _silu(z_ref[i, :, vs[h]].astype(f32))
        o_ref[i, :, vs[h]] = (o_h * ng_ref[...] * gate_z).astype(bf16)


def _gla(p, w2_pad, g_bias, norm_g, ltri, batch, seq):
    nc = seq // CHUNK
    slab_col = T_SMALL * (D // LANES) + OFF_SLAB // LANES
    tile = lambda t: pl.BlockSpec((PAIR, CHUNK, D), lambda b, c: (b, c, t))
    const = lambda shape: pl.BlockSpec(shape, lambda b, c: (0, 0))
    return pl.pallas_call(
        _gla_kernel,
        grid=(batch // PAIR, nc),
        in_specs=[tile(T_GQK), tile(T_GV),
                  pl.BlockSpec((PAIR, CHUNK, LANES), lambda b, c: (b, c, slab_col)),
                  tile(T_GZ),
                  const(w2_pad.shape), const(g_bias.shape), const(norm_g.shape),
                  const(ltri.shape)],
        out_specs=pl.BlockSpec((PAIR, CHUNK, D), lambda b, c: (b, c, 0)),
        out_shape=jax.ShapeDtypeStruct((batch, seq, D), bf16),
        scratch_shapes=[pltpu.VMEM((PAIR, GLA_HEADS, GLA_DV, GLA_DK), f32)],
        compiler_params=pltpu.CompilerParams(
            dimension_semantics=("parallel", "arbitrary"), vmem_limit_bytes=VMEM_LIMIT),
        name="gla_chunk",
    )(p, p, p, p, w2_pad, g_bias, norm_g, ltri)


def _merge_kernel(om_ref, or_ref, og_ref, ga_ref, gb_ref, gc_ref, x_ref, gate_ref, np_ref,
                  wm_ref, wr_ref, wg_ref, wo_ref, o_ref):
    gate_of = lambda ref: _sigmoid(ref[...].astype(f32))
    merged = (gate_of(ga_ref) * jnp.dot(om_ref[...], wm_ref[...], preferred_element_type=f32)
              + gate_of(gb_ref) * jnp.dot(or_ref[...], wr_ref[...], preferred_element_type=f32)
              + gate_of(gc_ref) * jnp.dot(og_ref[...], wg_ref[...], preferred_element_type=f32))
    y = jnp.dot(merged.astype(bf16), wo_ref[...], preferred_element_type=f32)
    o_ref[...] = x_ref[...] + gate_ref[0] * _rms(y, np_ref[...])


def _merge(o_mla, o_rwkv, o_gla, p, x2d, gate, norm_post, w_mla, w_rwkv, w_gla, w_out, seq):
    tokens = x2d.shape[0]
    tm = min(512, seq)
    per_b = seq // tm
    row = lambda i: (i, 0)
    const = lambda i: (0, 0)
    return pl.pallas_call(
        _merge_kernel,
        grid=(tokens // tm,),
        in_specs=[pl.BlockSpec((tm, D), row), pl.BlockSpec((tm, D), row),
                  pl.BlockSpec((tm, D), row),
                  pl.BlockSpec((tm, D), lambda i: (i, T_GA)),
                  pl.BlockSpec((tm, D), lambda i: (i, T_GB)),
                  pl.BlockSpec((tm, D), lambda i: (i, T_GC)),
                  pl.BlockSpec((tm, D), row),
                  pl.BlockSpec((1, 1, D), lambda i: (i // per_b, 0, 0)),
                  pl.BlockSpec((1, D), const),
                  pl.BlockSpec((D, D), const), pl.BlockSpec((D, D), const),
                  pl.BlockSpec((D, D), const), pl.BlockSpec((D, D), const)],
        out_specs=pl.BlockSpec((tm, D), row),
        out_shape=jax.ShapeDtypeStruct((tokens, D), f32),
        compiler_params=pltpu.CompilerParams(
            dimension_semantics=("parallel",), vmem_limit_bytes=VMEM_LIMIT),
        name="merge_outproj",
    )(o_mla, o_rwkv, o_gla, p, p, p, x2d, gate, norm_post, w_mla, w_rwkv, w_gla, w_out)


def _rot_half_cols(w):
    half = w.shape[-1] // 2
    return jnp.concatenate([-w[..., half:], w[..., :half]], axis=-1)


def _permute_w_in(w):
    sizes = (512, 256, 64, 1024, 1024, 1024, 1024, 64, 64, 1024, 512, 512, 1024, 16, 1024,
             1024, 1024, 1024)
    parts = []
    start = 0
    for size in sizes:
        parts.append(w[:, start:start + size])
        start += size
    (c_q, c_kv, k_rope, mla_z, r, k, v, w_l, a_l, r_z, g_q, g_k, g_v, g_l, g_z,
     gate_a, gate_b, gate_c) = parts
    pad = jnp.zeros((w.shape[0], OFF_LORA - OFF_SLAB - MLA_ROPE - g_l.shape[1]), w.dtype)
    return jnp.concatenate([c_q, c_kv, k_rope, g_l, pad, w_l, a_l, mla_z, r, k, v, r_z,
                            g_q, g_k, g_v, g_z, gate_a, gate_b, gate_c], axis=1).astype(bf16)


def _mla_weights(w_uq, w_ukv):
    n_in = w_uq.shape[0]
    wq = w_uq.reshape(n_in, MLA_HEADS, MLA_NOPE + MLA_ROPE)
    wq = jnp.concatenate([wq, _rot_half_cols(wq[..., MLA_NOPE:])], axis=-1)
    wkv = w_ukv.reshape(w_ukv.shape[0], MLA_HEADS, MLA_NOPE + MLA_V)
    return wq.transpose(1, 0, 2).astype(bf16), wkv.transpose(1, 0, 2).astype(bf16)


def kernel(x, c, positions, ada_w, ada_b, norm_pre, norm_post, w_in, rwkv_mu, mla_q_norm,
           mla_kv_norm, mla_w_uq, mla_w_ukv, mla_w_o, rwkv_w0, rwkv_w2, rwkv_a0, rwkv_a2,
           rwkv_k_k, rwkv_k_a, rwkv_r_k, rwkv_ln_w, rwkv_ln_b, rwkv_w_o, gla_w2, gla_b,
           gla_norm, gla_w_o, w_out):
    batch, seq, _ = x.shape
    assert batch % PAIR == 0 and seq % (2 * CHUNK) == 0, (batch, seq)
    depth = w_in.shape[0]
    tokens = batch * seq
    x2d = x.reshape(tokens, D)
    rope = _rope_table(positions, tokens)
    idx = jnp.arange(QUAD)
    seg = (idx[:, None] // RWKV_HEAD == idx[None, :] // RWKV_HEAD).astype(bf16)
    tri = jnp.arange(PAIR * CHUNK)
    ltri = ((tri[None, :] <= tri[:, None])
            & (tri[None, :] // CHUNK == tri[:, None] // CHUNK)).astype(bf16)
    row1 = lambda t: t.reshape(1, -1)

    mod = _modulation(c, ada_w, ada_b)
    for l in range(depth):
        shift = mod[l, :, :D].reshape(batch, 1, D)
        scale = mod[l, :, D:2 * D].reshape(batch, 1, D)
        gate = mod[l, :, 2 * D:].reshape(batch, 1, D)
        p = _inproj(x2d, scale, shift, row1(norm_pre[l]), _permute_w_in(w_in[l]), seq)

        wq, wkv = _mla_weights(mla_w_uq[l], mla_w_ukv[l])
        q, k, v = _mla_prep(p, rope, row1(mla_q_norm[l]), row1(mla_kv_norm[l]), wq, wkv,
                            batch, seq)
        o_mla = _flash(q, k, v, p, batch, seq).reshape(tokens, D)

        mu = rwkv_mu[l]
        mu_rkv = jnp.zeros((8, D), f32).at[:3].set(mu[:3 * D].reshape(3, D))
        w_lora = jnp.zeros((LANES, 2 * D), f32)
        w_lora = w_lora.at[:LANES // 2, :D].set(rwkv_w2[l]).at[LANES // 2:, D:].set(rwkv_a2[l])
        p3 = p.reshape(batch, seq, P_COLS)
        o_rwkv = _rwkv(p3, mu_rkv, row1(mu[3 * D:]), row1(rwkv_w0[l]), row1(rwkv_a0[l]),
                       w_lora.astype(bf16), row1(rwkv_k_k[l]), row1(rwkv_k_a[l]),
                       row1(rwkv_r_k[l]), row1(rwkv_ln_w[l]), row1(rwkv_ln_b[l]), seg, ltri,
                       batch, seq).reshape(tokens, D)

        w2_pad = jnp.zeros((LANES, GLA_HEADS * GLA_DK), f32)
        w2_pad = w2_pad.at[OFF_GL:OFF_GL + gla_w2.shape[1]].set(gla_w2[l]).astype(bf16)
        o_gla = _gla(p3, w2_pad, row1(gla_b[l]), row1(gla_norm[l]), ltri, batch,
                     seq).reshape(tokens, D)

        x2d = _merge(o_mla, o_rwkv, o_gla, p, x2d, gate, row1(norm_post[l]),
                     mla_w_o[l].astype(bf16), rwkv_w_o[l].astype(bf16),
                     gla_w_o[l].astype(bf16), w_out[l].astype(bf16), seq)
    return x2d.reshape(batch, seq, D)
```

```python
import functools

import jax
import jax.numpy as jnp
from jax import lax
from jax.experimental import pallas as pl
from jax.experimental.pallas import tpu as pltpu

f32 = jnp.float32
bf16 = jnp.bfloat16

D = 1024
CHUNK = 64
PAIR = 4
NORM_EPS = 1e-6
MLA_HEADS = 8
MLA_NOPE = 128
MLA_ROPE = 64
MLA_V = 128
MLA_QK_PAD = 256
FLASH_HEADS = 2
ROPE_BASE = 10000.0
LOG2_E = 1.4426950408889634
RWKV_HEAD = 64
RWKV_GN_EPS = 64e-5
GLA_HEADS = 4
GLA_DK = 128
GLA_DV = 256
GLA_GATE_NORM = 16.0
LANES = 128
SUB = 8
QUAD = 128
N_QUADS = D // QUAD
P_COLS = 12 * D
VMEM_LIMIT = 56 * 1024 * 1024

T_SMALL, T_MLA_Z, T_R, T_K, T_V, T_RZ, T_GQK, T_GV, T_GZ, T_GA, T_GB, T_GC = range(12)
OFF_CKV = 512
OFF_SLAB = 768
OFF_GL = 64
OFF_LORA = 896

NN_DIMS = (((1,), (0,)), ((), ()))
NT_DIMS = (((1,), (1,)), ((), ()))
TN_DIMS = (((0,), (0,)), ((), ()))


def _div_pow2(x, n):
    return lax.shift_right_logical(x, jnp.int32(n.bit_length() - 1))


def _softplus(x):
    return jnp.maximum(x, 0.0) + jnp.log(1.0 + jnp.exp(-jnp.abs(x)))


def _sigmoid(x):
    return 1.0 / (1.0 + jnp.exp(-x))


def _silu(x):
    return x * _sigmoid(x)


def _split_bf16(x):
    hi = x.astype(bf16)
    return hi, (x - hi.astype(f32)).astype(bf16)


def _seg_sums(xs, seg):
    rows = xs[0].shape[0]
    out = jnp.dot(jnp.concatenate(xs, axis=0).astype(bf16), seg, preferred_element_type=f32)
    return [out[i * rows:(i + 1) * rows] for i in range(len(xs))]


def _chunk_cumsum(ltri, x):
    hi, lo = _split_bf16(x)
    return (jnp.dot(ltri, hi, preferred_element_type=f32)
            + jnp.dot(ltri, lo, preferred_element_type=f32))


def _rms(x, gain):
    return x * lax.rsqrt(jnp.mean(x * x, axis=-1, keepdims=True) + NORM_EPS) * gain


def _mod_kernel(c_ref, w_ref, b_ref, o_ref):
    c = _silu(c_ref[...]).astype(bf16)
    o_ref[0] = jnp.dot(c, w_ref[0].astype(bf16), preferred_element_type=f32) + b_ref[0]


def _modulation(c, ada_w, ada_b):
    depth, _, n3 = ada_w.shape
    batch = c.shape[0]
    return pl.pallas_call(
        _mod_kernel,
        grid=(depth, n3 // D),
        in_specs=[pl.BlockSpec((batch, D), lambda l, j: (0, 0)),
                  pl.BlockSpec((1, D, D), lambda l, j: (l, 0, j)),
                  pl.BlockSpec((1, 1, D), lambda l, j: (l, 0, j))],
        out_specs=pl.BlockSpec((1, batch, D), lambda l, j: (l, 0, j)),
        out_shape=jax.ShapeDtypeStruct((depth, batch, n3), f32),
        name="adaln_mod",
    )(c, ada_w, ada_b.reshape(depth, 1, n3))


def _inproj_kernel(x_ref, scale_ref, shift_ref, g_ref, w_ref, o_ref, h_ref):
    @pl.when(pl.program_id(1) == 0)
    def _():
        h = _rms(x_ref[...], g_ref[...]) * (1.0 + scale_ref[0]) + shift_ref[0]
        h_ref[...] = h.astype(bf16)

    o_ref[...] = jnp.dot(h_ref[...], w_ref[...], preferred_element_type=f32).astype(bf16)


def _inproj(x2d, scale, shift, gain, w, seq):
    tokens = x2d.shape[0]
    tm = min(2048, seq)
    tn = D
    per_b = seq // tm
    return pl.pallas_call(
        _inproj_kernel,
        grid=(tokens // tm, P_COLS // tn),
        in_specs=[pl.BlockSpec((tm, D), lambda i, j: (i, 0)),
                  pl.BlockSpec((1, 1, D), lambda i, j: (i // per_b, 0, 0)),
                  pl.BlockSpec((1, 1, D), lambda i, j: (i // per_b, 0, 0)),
                  pl.BlockSpec((1, D), lambda i, j: (0, 0)),
                  pl.BlockSpec((D, tn), lambda i, j: (0, j))],
        out_specs=pl.BlockSpec((tm, tn), lambda i, j: (i, j)),
        out_shape=jax.ShapeDtypeStruct((tokens, P_COLS), bf16),
        scratch_shapes=[pltpu.VMEM((tm, D), bf16)],
        compiler_params=pltpu.CompilerParams(
            dimension_semantics=("parallel", "arbitrary"), vmem_limit_bytes=VMEM_LIMIT),
        name="norm_inproj",
    )(x2d, scale, shift, gain, w)


def _rope_kernel(pos_ref, invf_ref, cos_ref, sin_ref):
    ang = pos_ref[...].astype(f32) * invf_ref[...]
    cos_ref[...] = jnp.cos(ang)
    sin_ref[...] = jnp.sin(ang)


def _rope_table(positions, tokens):
    half = MLA_ROPE // 2
    pack = LANES // half
    rows = tokens // pack
    tr = min(2048, rows)
    inv_freq = ROPE_BASE ** (-jnp.arange(0, MLA_ROPE, 2, dtype=f32) / MLA_ROPE)
    invf = jnp.tile(inv_freq, pack).reshape(1, LANES)
    pos = jnp.repeat(positions.reshape(rows, pack), half, axis=1)
    cos, sin = pl.pallas_call(
        _rope_kernel,
        grid=(rows // tr,),
        in_specs=[pl.BlockSpec((tr, LANES), lambda i: (i, 0)),
                  pl.BlockSpec((1, LANES), lambda i: (0, 0))],
        out_specs=[pl.BlockSpec((tr, LANES), lambda i: (i, 0))] * 2,
        out_shape=[jax.ShapeDtypeStruct((rows, LANES), f32)] * 2,
        name="rope_table",
    )(pos, invf)
    cos, sin = cos.reshape(tokens, half), sin.reshape(tokens, half)
    return jnp.concatenate([cos, cos, sin, sin], axis=1)


def _mla_prep_kernel(p_ref, rope_ref, qn_ref, kvn_ref, wq_ref, wkv_ref, q_ref, k_ref, v_ref):
    p = p_ref[...].astype(f32)
    tm = p.shape[0]
    tv = v_ref.shape[-1]
    cq = _rms(p[:, :OFF_CKV], qn_ref[...]).astype(bf16)
    ckv = _rms(p[:, OFF_CKV:OFF_SLAB], kvn_ref[...]).astype(bf16)
    slab = p[:, OFF_SLAB:OFF_SLAB + LANES]
    cos_sin = rope_ref[...]
    lane = lax.broadcasted_iota(jnp.int32, (tm, LANES), 1)
    half = MLA_ROPE // 2
    rot = jnp.where(lane < half, -pltpu.roll(slab, LANES - half, 1), pltpu.roll(slab, half, 1))
    in_rope = lane < MLA_ROPE
    k_rope = slab * cos_sin + rot * pltpu.roll(cos_sin, MLA_ROPE, 1)
    k_rope = jnp.where(in_rope, k_rope, 0.0).astype(bf16)
    scale = (MLA_NOPE + MLA_ROPE) ** -0.5 * LOG2_E
    for h in range(MLA_HEADS):
        rq = jnp.dot(cq, wq_ref[h], preferred_element_type=f32)
        a = rq[:, MLA_NOPE:] * cos_sin
        q_rope = a + pltpu.roll(a, MLA_ROPE, 1)
        q_ref[0, h, :, :MLA_NOPE] = (rq[:, :MLA_NOPE] * scale).astype(bf16)
        q_ref[0, h, :, MLA_NOPE:] = jnp.where(in_rope, q_rope * scale, 0.0).astype(bf16)
        rkv = jnp.dot(ckv, wkv_ref[h], preferred_element_type=f32)
        k_ref[0, h, :, :MLA_NOPE] = rkv[:, :MLA_NOPE].astype(bf16)
        k_ref[0, h, :, MLA_NOPE:] = k_rope
        v_t = rkv[:, MLA_NOPE:].T
        for j in range(v_ref.shape[2]):
            v_ref[0, h, j] = v_t[:, j * tv:(j + 1) * tv].astype(bf16)


def _mla_prep(p, rope, q_norm, kv_norm, wq, wkv, batch, seq):
    tm = min(512, seq)
    tv = _flash_tile(seq)
    per_b = seq // tm
    dqk = MLA_QK_PAD
    const2 = lambda b, i: (0, 0)
    const3 = lambda b, i: (0, 0, 0)
    return pl.pallas_call(
        _mla_prep_kernel,
        grid=(batch, per_b),
        in_specs=[pl.BlockSpec((tm, D), lambda b, i: (b * per_b + i, T_SMALL)),
                  pl.BlockSpec((tm, LANES), lambda b, i: (b * per_b + i, 0)),
                  pl.BlockSpec((1, OFF_CKV), const2),
                  pl.BlockSpec((1, OFF_SLAB - OFF_CKV), const2),
                  pl.BlockSpec(wq.shape, const3),
                  pl.BlockSpec(wkv.shape, const3)],
        out_specs=[pl.BlockSpec((1, MLA_HEADS, tm, dqk), lambda b, i: (b, 0, i, 0)),
                   pl.BlockSpec((1, MLA_HEADS, tm, dqk), lambda b, i: (b, 0, i, 0)),
                   pl.BlockSpec((1, MLA_HEADS, tm // tv, MLA_V, tv),
                                lambda b, i: (b, 0, i, 0, 0))],
        out_shape=[jax.ShapeDtypeStruct((batch, MLA_HEADS, seq, dqk), bf16),
                   jax.ShapeDtypeStruct((batch, MLA_HEADS, seq, dqk), bf16),
                   jax.ShapeDtypeStruct((batch, MLA_HEADS, seq // tv, MLA_V, tv), bf16)],
        compiler_params=pltpu.CompilerParams(
            dimension_semantics=("parallel", "parallel"), vmem_limit_bytes=VMEM_LIMIT),
        name="mla_prep",
    )(p, rope, q_norm, kv_norm, wq, wkv)


def _flash_tile(seq):
    return min(512, seq // 2)


def _flash_kernel(q_ref, k_ref, v_ref, z_ref, o_ref, m_ref, l_ref, acc_ref, *, th):
    qi = pl.program_id(2)
    m_ref[...] = jnp.full(m_ref.shape, -jnp.inf, f32)
    l_ref[...] = jnp.zeros(l_ref.shape, f32)
    acc_ref[...] = jnp.zeros(acc_ref.shape, f32)
    groups = th // SUB

    def run(chains):
        slot = [2 * h + half for h, half, _, _ in chains]
        s = [lax.dot_general(k_ref[0, h, pl.ds(pl.multiple_of(ki * th, th), th), :],
                             q_ref[0, h, half * th:(half + 1) * th, :], NT_DIMS,
                             preferred_element_type=f32)
             for h, half, ki, _ in chains]
        if any(diag for _, _, _, diag in chains):
            k_chunk = _div_pow2(lax.broadcasted_iota(jnp.int32, (th, th), 0), CHUNK)
            q_chunk = _div_pow2(lax.broadcasted_iota(jnp.int32, (th, th), 1), CHUNK)
            allowed = k_chunk <= q_chunk
            s = [jnp.where(allowed, x, -jnp.inf) if diag else x
                 for x, (_, _, _, diag) in zip(s, chains)]
        s = [x.reshape(groups, SUB, th) for x in s]
        m_prev = [m_ref[i] for i in slot]
        m_new = [jnp.maximum(mp, jnp.max(jnp.max(x, axis=0), axis=0, keepdims=True))
                 for mp, x in zip(m_prev, s)]
        alpha = [jnp.exp2(mp - mn) for mp, mn in zip(m_prev, m_new)]
        prob = [jnp.exp2(x - mn[None]) for x, mn in zip(s, m_new)]
        for i, al, pr in zip(slot, alpha, prob):
            l_ref[i] = al * l_ref[i] + jnp.sum(pr, axis=0)
        pv = [jnp.dot(v_ref[0, h, ki], pr.reshape(th, th).astype(bf16),
                      preferred_element_type=f32)
              for pr, (h, _, ki, _) in zip(prob, chains)]
        for i, al, x, mn in zip(slot, alpha, pv, m_new):
            acc = acc_ref[i].reshape(MLA_V // SUB, SUB, th) * al[None]
            acc_ref[i] = acc.reshape(MLA_V, th) + x
            m_ref[i] = mn

    heads = range(FLASH_HEADS)
    n_full = 2 * qi

    def body(pair_idx, carry):
        for ki in (2 * pair_idx, 2 * pair_idx + 1):
            run([(h, half, ki, False) for h in heads for half in range(2)])
        return carry

    lax.fori_loop(0, qi, body, 0)
    run([(h, half, n_full, half == 0) for h in heads for half in range(2)])
    run([(h, 1, n_full + 1, True) for h in heads])
    for h in heads:
        lanes = slice(h * MLA_V, (h + 1) * MLA_V)
        for half in range(2):
            rows = slice(half * th, (half + 1) * th)
            i = 2 * h + half
            inv_l = 1.0 / jnp.sum(l_ref[i], axis=0, keepdims=True)
            o = (acc_ref[i] * inv_l).T
            o_ref[0, rows, lanes] = (o * _silu(z_ref[rows, lanes].astype(f32))).astype(bf16)


def _flash(q, k, v_t, p, batch, seq):
    th = _flash_tile(seq)
    tq = 2 * th
    nq = seq // tq
    dqk = MLA_QK_PAD
    hv = FLASH_HEADS * MLA_V
    z_col0 = T_MLA_Z * (D // hv)
    n_chain = 2 * FLASH_HEADS
    return pl.pallas_call(
        functools.partial(_flash_kernel, th=th),
        grid=(batch, MLA_HEADS // FLASH_HEADS, nq),
        in_specs=[pl.BlockSpec((1, FLASH_HEADS, tq, dqk), lambda b, h, i: (b, h, i, 0)),
                  pl.BlockSpec((1, FLASH_HEADS, seq, dqk), lambda b, h, i: (b, h, 0, 0)),
                  pl.BlockSpec((1, FLASH_HEADS, seq // th, MLA_V, th),
                               lambda b, h, i: (b, h, 0, 0, 0)),
                  pl.BlockSpec((tq, hv), lambda b, h, i: (b * nq + i, z_col0 + h))],
        out_specs=pl.BlockSpec((1, tq, hv), lambda b, h, i: (b, i, h)),
        out_shape=jax.ShapeDtypeStruct((batch, seq, MLA_HEADS * MLA_V), bf16),
        scratch_shapes=[pltpu.VMEM((n_chain, SUB, th), f32),
                        pltpu.VMEM((n_chain, SUB, th), f32),
                        pltpu.VMEM((n_chain, MLA_V, th), f32)],
        compiler_params=pltpu.CompilerParams(
            dimension_semantics=("parallel", "parallel", "arbitrary"),
            vmem_limit_bytes=VMEM_LIMIT),
        name="mla_flash",
    )(q, k, v_t, p)


def _shift_mix(s, prev_rows, mu):
    rolled = pltpu.roll(s, 1, 0)
    sub = SUB
    first = lax.broadcasted_iota(jnp.int32, (sub, s.shape[1]), 0) == 0
    pieces = []
    for i, prev_row in enumerate(prev_rows):
        top = i * CHUNK
        pieces += [jnp.where(first, prev_row, rolled[top:top + sub]),
                   rolled[top + sub:top + CHUNK]]
    prev = jnp.concatenate(pieces, axis=0)
    return s + mu * (prev - s)


def _rwkv_kernel(r_ref, k_ref, v_ref, lora_ref, z_ref, mu_ref, mul_ref, w0_ref, a0_ref, wl_ref,
                 kk_ref, ka_ref, rk_ref, lnw_ref, lnb_ref, seg_ref, ltri_ref, eye_ref, hm_ref,
                 o_ref, state_ref, prev_ref, prevl_ref):
    c_len = CHUNK
    halves = range(PAIR)

    @pl.when(pl.program_id(1) == 0)
    def _():
        state_ref[...] = jnp.zeros(state_ref.shape, f32)
        prev_ref[...] = jnp.zeros(prev_ref.shape, f32)
        prevl_ref[...] = jnp.zeros(prevl_ref.shape, f32)

    def load(ref):
        x = ref[...].astype(f32)
        return x.reshape(PAIR * c_len, x.shape[-1])

    def last_rows(x):
        return [x[(i + 1) * c_len - 1:(i + 1) * c_len, :] for i in halves]

    streams = [load(ref) for ref in (r_ref, k_ref, v_ref)]
    lora_raw = load(lora_ref)
    r, k, v = (_shift_mix(x, [prev_ref[i, j:j + 1, :] for i in halves], mu_ref[j:j + 1, :])
               for j, x in enumerate(streams))
    lora = _shift_mix(lora_raw, [prevl_ref[i, 0:1, :] for i in halves], mul_ref[...])
    for j, x in enumerate(streams):
        for i, last in enumerate(last_rows(x)):
            prev_ref[i, j:j + 1, :] = last
    for i, last in enumerate(last_rows(lora_raw)):
        prevl_ref[i, 0:1, :] = last

    lane = lax.broadcasted_iota(jnp.int32, lora.shape, 1)
    lora = jnp.where(lane < LANES // 2, jnp.tanh(lora), lora)
    proj = jnp.dot(lora.astype(bf16), wl_ref[...], preferred_element_type=f32)
    z = w0_ref[...] + proj[:, :D]
    w_log2 = ((jnp.minimum(z, 0.0) - 0.5) * LOG2_E
              - jnp.log2(1.0 + jnp.exp2(jnp.abs(z) * -LOG2_E)))
    log2_decay = jnp.exp2(w_log2) * -LOG2_E
    a = _sigmoid(a0_ref[...] + proj[:, D:])
    kk = k * kk_ref[...]
    ka = ka_ref[...]
    k = k * (a * ka + (1.0 - ka))

    cum = _chunk_cumsum(ltri_ref[...], log2_decay)
    g_inc = jnp.exp2(cum)
    g_inv = jnp.exp2(-cum)
    g_exc = jnp.exp2(cum - log2_decay)
    g_end = [jnp.exp2(last) for last in last_rows(cum)]

    seg = seg_ref[...]
    row = lax.broadcasted_iota(jnp.int32, (c_len, QUAD), 0)
    col = lax.broadcasted_iota(jnp.int32, (c_len, QUAD), 1)
    col_t = jnp.bitwise_and(col, c_len - 1)
    strict = col_t < row
    incl = col_t <= row
    eye = eye_ref[...]
    head_masks = [hm_ref[g] for g in range(QUAD // RWKV_HEAD)]
    block_diag = (_div_pow2(lax.broadcasted_iota(jnp.int32, (QUAD, QUAD), 0), RWKV_HEAD)
                  == _div_pow2(lax.broadcasted_iota(jnp.int32, (QUAD, QUAD), 1), RWKV_HEAD))

    def stack(y):
        y = y.astype(bf16)
        return jnp.concatenate([y * m for m in head_masks], axis=0)

    def bdot(x, y, dims=NN_DIMS):
        return lax.dot_general(x.astype(bf16), y.astype(bf16), dims, preferred_element_type=f32)

    def qprod(x, y):
        return bdot(x, stack(y))

    slabs = [slice(qd * QUAD, (qd + 1) * QUAD) for qd in range(N_QUADS)]
    items = [(i, qd) for i in halves for qd in range(N_QUADS)]
    each = lambda fn, *cols: [fn(*args) for args in zip(*cols)]
    cut = lambda x: [x[i * c_len:(i + 1) * c_len, slabs[qd]] for i, qd in items]

    kk_q = cut(kk)
    kk_sq = _seg_sums([x * x for x in kk_q], seg)
    kk_q = each(lambda x, sq: x * lax.rsqrt(jnp.maximum(sq, 1e-24)), kk_q, kk_sq)
    a_bar = each(lambda x, g: -x * g, kk_q, cut(g_exc))
    b_til = each(lambda x, a_q, g: x * a_q * g, kk_q, cut(a), cut(g_inv))
    k_til = each(jnp.multiply, cut(k), cut(g_inv))
    r_bar = each(jnp.multiply, cut(r), cut(g_inc))
    g_end_q = [g_end[i][:, slabs[qd]] for i, qd in items]
    b_end = each(jnp.multiply, b_til, g_end_q)
    k_end = each(jnp.multiply, k_til, g_end_q)
    v_q = cut(v)

    pair = each(lambda ab, rb, bt, kt: bdot(jnp.concatenate([ab, rb], axis=0),
                                            jnp.concatenate([stack(bt), stack(kt)], axis=0),
                                            NT_DIMS),
                a_bar, r_bar, b_til, k_til)
    a_ab = [jnp.where(strict, x[:c_len, :QUAD], 0.0) for x in pair]
    a_ak = [jnp.where(strict, x[:c_len, QUAD:], 0.0) for x in pair]
    a_rb = [jnp.where(incl, x[c_len:, :QUAD], 0.0) for x in pair]
    a_rk = [jnp.where(incl, x[c_len:, QUAD:], 0.0) for x in pair]

    t_inv = [eye + x for x in a_ab]
    power = each(qprod, a_ab, a_ab)
    span = 4
    while span < c_len:
        both = each(lambda t, pw: qprod(jnp.concatenate([t, pw], axis=0), pw), t_inv, power)
        t_inv = each(lambda t, x: t + x[:c_len], t_inv, both)
        power = [x[c_len:] for x in both]
        span *= 2
    t_inv = each(lambda t, pw: t + qprod(t, pw), t_inv, power)

    akv = each(lambda aak, ark, vq: qprod(jnp.concatenate([aak, ark], axis=0), vq),
               a_ak, a_rk, v_q)
    wu = each(lambda t, ab, x: bdot(t, jnp.concatenate([stack(ab), stack(x[:c_len])], axis=1)),
              t_inv, a_bar, akv)
    state = [state_ref[i, qd] for i, qd in items]
    on_state = each(lambda x, rb, st: bdot(jnp.concatenate([x[:, :QUAD], rb], axis=0), st,
                                           NT_DIMS), wu, r_bar, state)
    u = each(lambda hs, x: hs[:c_len] + x[:, QUAD:], on_state, wu)
    y = each(lambda hs, arb, u_q, x: hs[c_len:] + qprod(arb, u_q) + x[c_len:],
             on_state, a_rb, u, akv)
    upd = each(lambda u_q, vq, be, ke: bdot(jnp.concatenate([u_q, vq], axis=0),
                                            jnp.concatenate([be, ke], axis=0), TN_DIMS),
               u, v_q, b_end, k_end)
    for n, (i, qd) in enumerate(items):
        state_ref[i, qd] = state[n] * g_end_q[n] + jnp.where(block_diag, upd[n], 0.0)

    inv_n = 1.0 / RWKV_HEAD
    rkr = cut(r * k * rk_ref[...])
    sums = _seg_sums(y + rkr, seg)
    dev = each(lambda y_q, total: y_q - total * inv_n, y, sums[:len(items)])
    bonus = each(jnp.multiply, sums[len(items):], v_q)
    var = [total * inv_n for total in _seg_sums([d_q * d_q for d_q in dev], seg)]
    for n, (i, qd) in enumerate(items):
        sl = slabs[qd]
        y_q = dev[n] * lax.rsqrt(var[n] + RWKV_GN_EPS) * lnw_ref[:, sl] + lnb_ref[:, sl]
        o_ref[i, :, sl] = ((y_q + bonus[n]) * _silu(z_ref[i, :, sl].astype(f32))).astype(bf16)


def _rwkv(p, mu_rkv, mu_lora, w0, a0, w_lora, k_k, k_a, r_k, ln_w, ln_b, seg, ltri, batch, seq):
    nc = seq // CHUNK
    col = jnp.arange(QUAD)
    eye = (col[None, :] % CHUNK == jnp.arange(CHUNK)[:, None]).astype(f32)
    head_masks = jnp.broadcast_to(
        (col[None, None, :] // RWKV_HEAD == jnp.arange(QUAD // RWKV_HEAD)[:, None, None]),
        (QUAD // RWKV_HEAD, CHUNK, QUAD)).astype(bf16)
    lora_col = T_SMALL * (D // LANES) + OFF_LORA // LANES
    tile = lambda t: pl.BlockSpec((PAIR, CHUNK, D), lambda b, c: (b, c, t))
    const = lambda shape: pl.BlockSpec(shape, lambda b, c: (0, 0))
    return pl.pallas_call(
        _rwkv_kernel,
        grid=(batch // PAIR, nc),
        in_specs=[tile(T_R), tile(T_K), tile(T_V),
                  pl.BlockSpec((PAIR, CHUNK, LANES), lambda b, c: (b, c, lora_col)),
                  tile(T_RZ),
                  const(mu_rkv.shape), const(mu_lora.shape), const((1, D)), const((1, D)),
                  const(w_lora.shape),
                  const((1, D)), const((1, D)), const((1, D)), const((1, D)), const((1, D)),
                  const(seg.shape), const(ltri.shape), const(eye.shape),
                  pl.BlockSpec(head_masks.shape, lambda b, c: (0, 0, 0))],
        out_specs=pl.BlockSpec((PAIR, CHUNK, D), lambda b, c: (b, c, 0)),
        out_shape=jax.ShapeDtypeStruct((batch, seq, D), bf16),
        scratch_shapes=[pltpu.VMEM((PAIR, N_QUADS, QUAD, QUAD), f32),
                        pltpu.VMEM((PAIR, 8, D), f32), pltpu.VMEM((PAIR, 8, LANES), f32)],
        compiler_params=pltpu.CompilerParams(
            dimension_semantics=("parallel", "arbitrary"), vmem_limit_bytes=VMEM_LIMIT),
        name="rwkv7_chunk",
    )(p, p, p, p, p, mu_rkv, mu_lora, w0, a0, w_lora, k_k, k_a, r_k, ln_w, ln_b, seg, ltri, eye,
      head_masks)


def _gla_kernel(qk_ref, v_ref, gl_ref, z_ref, w2_ref, gb_ref, ng_ref, ltri_ref, o_ref, state_ref):
    c_len = CHUNK
    kw = GLA_HEADS * GLA_DK
    halves = range(PAIR)

    @pl.when(pl.program_id(1) == 0)
    def _():
        state_ref[...] = jnp.zeros(state_ref.shape, f32)

    qk = qk_ref[...].astype(f32).reshape(PAIR * c_len, 2 * kw)
    q = qk[:, :kw] * (GLA_DK ** -0.5)
    k = qk[:, kw:]
    gl = gl_ref[...].reshape(PAIR * c_len, LANES)
    gate = jnp.dot(gl, w2_ref[...], preferred_element_type=f32) + gb_ref[...]
    log_a = -_softplus(-gate) / GLA_GATE_NORM
    cum = _chunk_cumsum(ltri_ref[...], log_a)
    rows = [slice(i * c_len, (i + 1) * c_len) for i in halves]
    cum_last = [cum[sl.stop - 1:sl.stop, :] for sl in rows]
    q_dec = (q * jnp.exp(cum)).astype(bf16)
    k_inv = (k * jnp.exp(-cum)).astype(bf16)
    k_end = [(k[sl] * jnp.exp(last - cum[sl])).astype(bf16) for sl, last in zip(rows, cum_last)]
    decay = [jnp.exp(last) for last in cum_last]
    causal = (lax.broadcasted_iota(jnp.int32, (c_len, c_len), 1)
              <= lax.broadcasted_iota(jnp.int32, (c_len, c_len), 0))
    ks = [slice(h * GLA_DK, (h + 1) * GLA_DK) for h in range(GLA_HEADS)]
    vs = [slice(h * GLA_DV, (h + 1) * GLA_DV) for h in range(GLA_HEADS)]
    items = [(i, h) for i in halves for h in range(GLA_HEADS)]
    v_h = [v_ref[i, :, vs[h]] for i, h in items]
    state = [state_ref[i, h] for i, h in items]
    attn = [lax.dot_general(q_dec[rows[i], ks[h]], k_inv[rows[i], ks[h]], NT_DIMS,
                            preferred_element_type=f32) for i, h in items]
    attn = [jnp.where(causal, x, 0.0).astype(bf16) for x in attn]
    o = [jnp.dot(attn[n], v_h[n], preferred_element_type=f32)
         + lax.dot_general(q_dec[rows[i], ks[h]], state[n].astype(bf16), NT_DIMS,
                           preferred_element_type=f32) for n, (i, h) in enumerate(items)]
    for n, (i, h) in enumerate(items):
        state_ref[i, h] = state[n] * decay[i][:, ks[h]] + lax.dot_general(
            v_h[n], k_end[i][:, ks[h]], TN_DIMS, preferred_element_type=f32)
    for n, (i, h) in enumerate(items):
        o_h = o[n] * lax.rsqrt(jnp.mean(o[n] * o[n], axis=-1, keepdims=True) + NORM_EPS)
        gate_z = _silu(z_ref[i, :, vs[h]].astype(f32))
        o_ref[i, :, vs[h]] = (o_h * ng_ref[...] * gate_z).astype(bf16)


def _gla(p, w2_pad, g_bias, norm_g, ltri, batch, seq):
    nc = seq // CHUNK
    slab_col = T_SMALL * (D // LANES) + OFF_SLAB // LANES
    tile = lambda t: pl.BlockSpec((PAIR, CHUNK, D), lambda b, c: (b, c, t))
    const = lambda shape: pl.BlockSpec(shape, lambda b, c: (0, 0))
    return pl.pallas_call(
        _gla_kernel,
        grid=(batch // PAIR, nc),
        in_specs=[tile(T_GQK), tile(T_GV),
                  pl.BlockSpec((PAIR, CHUNK, LANES), lambda b, c: (b, c, slab_col)),
                  tile(T_GZ),
                  const(w2_pad.shape), const(g_bias.shape), const(norm_g.shape),
                  const(ltri.shape)],
        out_specs=pl.BlockSpec((PAIR, CHUNK, D), lambda b, c: (b, c, 0)),
        out_shape=jax.ShapeDtypeStruct((batch, seq, D), bf16),
        scratch_shapes=[pltpu.VMEM((PAIR, GLA_HEADS, GLA_DV, GLA_DK), f32)],
        compiler_params=pltpu.CompilerParams(
            dimension_semantics=("parallel", "arbitrary"), vmem_limit_bytes=VMEM_LIMIT),
        name="gla_chunk",
    )(p, p, p, p, w2_pad, g_bias, norm_g, ltri)


def _merge_kernel(om_ref, or_ref, og_ref, ga_ref, gb_ref, gc_ref, x_ref, gate_ref, np_ref,
                  wm_ref, wr_ref, wg_ref, wo_ref, o_ref):
    gate_of = lambda ref: _sigmoid(ref[...].astype(f32))
    merged = (gate_of(ga_ref) * jnp.dot(om_ref[...], wm_ref[...], preferred_element_type=f32)
              + gate_of(gb_ref) * jnp.dot(or_ref[...], wr_ref[...], preferred_element_type=f32)
              + gate_of(gc_ref) * jnp.dot(og_ref[...], wg_ref[...], preferred_element_type=f32))
    y = jnp.dot(merged.astype(bf16), wo_ref[...], preferred_element_type=f32)
    o_ref[...] = x_ref[...] + gate_ref[0] * _rms(y, np_ref[...])


def _merge(o_mla, o_rwkv, o_gla, p, x2d, gate, norm_post, w_mla, w_rwkv, w_gla, w_out, seq):
    tokens = x2d.shape[0]
    tm = min(512, seq)
    per_b = seq // tm
    row = lambda i: (i, 0)
    const = lambda i: (0, 0)
    return pl.pallas_call(
        _merge_kernel,
        grid=(tokens // tm,),
        in_specs=[pl.BlockSpec((tm, D), row), pl.BlockSpec((tm, D), row),
                  pl.BlockSpec((tm, D), row),
                  pl.BlockSpec((tm, D), lambda i: (i, T_GA)),
                  pl.BlockSpec((tm, D), lambda i: (i, T_GB)),
                  pl.BlockSpec((tm, D), lambda i: (i, T_GC)),
                  pl.BlockSpec((tm, D), row),
                  pl.BlockSpec((1, 1, D), lambda i: (i // per_b, 0, 0)),
                  pl.BlockSpec((1, D), const),
                  pl.BlockSpec((D, D), const), pl.BlockSpec((D, D), const),
                  pl.BlockSpec((D, D), const), pl.BlockSpec((D, D), const)],
        out_specs=pl.BlockSpec((tm, D), row),
        out_shape=jax.ShapeDtypeStruct((tokens, D), f32),
        compiler_params=pltpu.CompilerParams(
            dimension_semantics=("parallel",), vmem_limit_bytes=VMEM_LIMIT),
        name="merge_outproj",
    )(o_mla, o_rwkv, o_gla, p, p, p, x2d, gate, norm_post, w_mla, w_rwkv, w_gla, w_out)


def _rot_half_cols(w):
    half = w.shape[-1] // 2
    return jnp.concatenate([-w[..., half:], w[..., :half]], axis=-1)


def _permute_w_in(w):
    sizes = (512, 256, 64, 1024, 1024, 1024, 1024, 64, 64, 1024, 512, 512, 1024, 16, 1024,
             1024, 1024, 1024)
    parts = []
    start = 0
    for size in sizes:
        parts.append(w[:, start:start + size])
        start += size
    (c_q, c_kv, k_rope, mla_z, r, k, v, w_l, a_l, r_z, g_q, g_k, g_v, g_l, g_z,
     gate_a, gate_b, gate_c) = parts
    pad = jnp.zeros((w.shape[0], OFF_LORA - OFF_SLAB - MLA_ROPE - g_l.shape[1]), w.dtype)
    return jnp.concatenate([c_q, c_kv, k_rope, g_l, pad, w_l, a_l, mla_z, r, k, v, r_z,
                            g_q, g_k, g_v, g_z, gate_a, gate_b, gate_c], axis=1).astype(bf16)


def _mla_weights(w_uq, w_ukv):
    n_in = w_uq.shape[0]
    wq = w_uq.reshape(n_in, MLA_HEADS, MLA_NOPE + MLA_ROPE)
    wq = jnp.concatenate([wq, _rot_half_cols(wq[..., MLA_NOPE:])], axis=-1)
    wkv = w_ukv.reshape(w_ukv.shape[0], MLA_HEADS, MLA_NOPE + MLA_V)
    return wq.transpose(1, 0, 2).astype(bf16), wkv.transpose(1, 0, 2).astype(bf16)


def kernel(x, c, positions, ada_w, ada_b, norm_pre, norm_post, w_in, rwkv_mu, mla_q_norm,
           mla_kv_norm, mla_w_uq, mla_w_ukv, mla_w_o, rwkv_w0, rwkv_w2, rwkv_a0, rwkv_a2,
           rwkv_k_k, rwkv_k_a, rwkv_r_k, rwkv_ln_w, rwkv_ln_b, rwkv_w_o, gla_w2, gla_b,
           gla_norm, gla_w_o, w_out):
    batch, seq, _ = x.shape
    assert batch % PAIR == 0 and seq % (2 * CHUNK) == 0, (batch, seq)
    depth = w_in.shape[0]
    tokens = batch * seq
    x2d = x.reshape(tokens, D)
    rope = _rope_table(positions, tokens)
    idx = jnp.arange(QUAD)
    seg = (idx[:, None] // RWKV_HEAD == idx[None, :] // RWKV_HEAD).astype(bf16)
    tri = jnp.arange(PAIR * CHUNK)
    ltri = ((tri[None, :] <= tri[:, None])
            & (tri[None, :] // CHUNK == tri[:, None] // CHUNK)).astype(bf16)
    row1 = lambda t: t.reshape(1, -1)

    mod = _modulation(c, ada_w, ada_b)
    for l in range(depth):
        shift = mod[l, :, :D].reshape(batch, 1, D)
        scale = mod[l, :, D:2 * D].reshape(batch, 1, D)
        gate = mod[l, :, 2 * D:].reshape(batch, 1, D)
        p = _inproj(x2d, scale, shift, row1(norm_pre[l]), _permute_w_in(w_in[l]), seq)

        wq, wkv = _mla_weights(mla_w_uq[l], mla_w_ukv[l])
        q, k, v = _mla_prep(p, rope, row1(mla_q_norm[l]), row1(mla_kv_norm[l]), wq, wkv,
                            batch, seq)
        o_mla = _flash(q, k, v, p, batch, seq).reshape(tokens, D)

        mu = rwkv_mu[l]
        mu_rkv = jnp.zeros((8, D), f32).at[:3].set(mu[:3 * D].reshape(3, D))
        w_lora = jnp.zeros((LANES, 2 * D), f32)
        w_lora = w_lora.at[:LANES // 2, :D].set(rwkv_w2[l]).at[LANES // 2:, D:].set(rwkv_a2[l])
        p3 = p.reshape(batch, seq, P_COLS)
        o_rwkv = _rwkv(p3, mu_rkv, row1(mu[3 * D:]), row1(rwkv_w0[l]), row1(rwkv_a0[l]),
                       w_lora.astype(bf16), row1(rwkv_k_k[l]), row1(rwkv_k_a[l]),
                       row1(rwkv_r_k[l]), row1(rwkv_ln_w[l]), row1(rwkv_ln_b[l]), seg, ltri,
                       batch, seq).reshape(tokens, D)

        w2_pad = jnp.zeros((LANES, GLA_HEADS * GLA_DK), f32)
        w2_pad = w2_pad.at[OFF_GL:OFF_GL + gla_w2.shape[1]].set(gla_w2[l]).astype(bf16)
        o_gla = _gla(p3, w2_pad, row1(gla_b[l]), row1(gla_norm[l]), ltri, batch,
                     seq).reshape(tokens, D)

        x2d = _merge(o_mla, o_rwkv, o_gla, p, x2d, gate, row1(norm_post[l]),
                     mla_w_o[l].astype(bf16), rwkv_w_o[l].astype(bf16),
                     gla_w_o[l].astype(bf16), w_out[l].astype(bf16), seq)
    return x2d.reshape(batch, seq, D)
```

```python
import functools

import jax
import jax.numpy as jnp
from jax import lax
from jax.experimental import pallas as pl
from jax.experimental.pallas import tpu as pltpu

f32 = jnp.float32
bf16 = jnp.bfloat16

D = 1024
CHUNK = 64
PAIR = 8
NORM_EPS = 1e-6
MLA_HEADS = 8
MLA_NOPE = 128
MLA_ROPE = 64
MLA_V = 128
MLA_QK_PAD = 256
FLASH_HEADS = 4
ROPE_BASE = 10000.0
LOG2_E = 1.4426950408889634
RWKV_HEAD = 64
RWKV_GN_EPS = 64e-5
GLA_HEADS = 4
GLA_DK = 128
GLA_DV = 256
GLA_GATE_NORM = 16.0
LANES = 128
SUB = 8
QUAD = 128
N_QUADS = D // QUAD
SEG_WIDTH = 256
P_COLS = 12 * D
VMEM_LIMIT = 56 * 1024 * 1024

T_SMALL, T_MLA_Z, T_R, T_K, T_V, T_RZ, T_GQK, T_GV, T_GZ, T_GA, T_GB, T_GC = range(12)
OFF_CKV = 512
OFF_SLAB = 768
OFF_GL = 64
OFF_LORA = 896

NN_DIMS = (((1,), (0,)), ((), ()))
NT_DIMS = (((1,), (1,)), ((), ()))
TN_DIMS = (((0,), (0,)), ((), ()))


def _div_pow2(x, n):
    return lax.shift_right_logical(x, jnp.int32(n.bit_length() - 1))


def _softplus(x):
    return jnp.maximum(x, 0.0) + jnp.log(1.0 + jnp.exp(-jnp.abs(x)))


def _sigmoid(x):
    return 1.0 / (1.0 + jnp.exp(-x))


def _silu(x):
    return x * _sigmoid(x)


def _split_bf16(x):
    hi = x.astype(bf16)
    return hi, (x - hi.astype(f32)).astype(bf16)


def _seg_sums(xs, seg):
    rows, width = xs[0].shape
    join = seg.shape[0] // width
    wide = [jnp.concatenate(xs[i:i + join], axis=1) for i in range(0, len(xs), join)]
    out = jnp.dot(jnp.concatenate(wide, axis=0).astype(bf16), seg, preferred_element_type=f32)
    return [out[(i // join) * rows:(i // join + 1) * rows,
                (i % join) * width:(i % join + 1) * width] for i in range(len(xs))]


def _chunk_cumsum(ltri, x):
    hi, lo = _split_bf16(x)
    size = ltri.shape[0]
    groups = [slice(g, g + size) for g in range(0, x.shape[0], size)]
    return jnp.concatenate(
        [jnp.dot(ltri, hi[sl], preferred_element_type=f32)
         + jnp.dot(ltri, lo[sl], preferred_element_type=f32) for sl in groups], axis=0)


def _rms(x, gain):
    return x * lax.rsqrt(jnp.mean(x * x, axis=-1, keepdims=True) + NORM_EPS) * gain


def _mod_kernel(c_ref, w_ref, b_ref, o_ref):
    c = _silu(c_ref[...]).astype(bf16)
    o_ref[0] = jnp.dot(c, w_ref[0].astype(bf16), preferred_element_type=f32) + b_ref[0]


def _modulation(c, ada_w, ada_b):
    depth, _, n3 = ada_w.shape
    batch = c.shape[0]
    return pl.pallas_call(
        _mod_kernel,
        grid=(depth, n3 // D),
        in_specs=[pl.BlockSpec((batch, D), lambda l, j: (0, 0)),
                  pl.BlockSpec((1, D, D), lambda l, j: (l, 0, j)),
                  pl.BlockSpec((1, 1, D), lambda l, j: (l, 0, j))],
        out_specs=pl.BlockSpec((1, batch, D), lambda l, j: (l, 0, j)),
        out_shape=jax.ShapeDtypeStruct((depth, batch, n3), f32),
        name="adaln_mod",
    )(c, ada_w, ada_b.reshape(depth, 1, n3))


def _inproj_kernel(x_ref, scale_ref, shift_ref, g_ref, w_ref, o_ref, h_ref):
    @pl.when(pl.program_id(1) == 0)
    def _():
        h = _rms(x_ref[...], g_ref[...]) * (1.0 + scale_ref[0]) + shift_ref[0]
        h_ref[...] = h.astype(bf16)

    o_ref[...] = jnp.dot(h_ref[...], w_ref[...], preferred_element_type=f32).astype(bf16)


def _inproj(x2d, scale, shift, gain, w, seq):
    tokens = x2d.shape[0]
    tm = min(2048, seq)
    tn = D
    per_b = seq // tm
    return pl.pallas_call(
        _inproj_kernel,
        grid=(tokens // tm, P_COLS // tn),
        in_specs=[pl.BlockSpec((tm, D), lambda i, j: (i, 0)),
                  pl.BlockSpec((1, 1, D), lambda i, j: (i // per_b, 0, 0)),
                  pl.BlockSpec((1, 1, D), lambda i, j: (i // per_b, 0, 0)),
                  pl.BlockSpec((1, D), lambda i, j: (0, 0)),
                  pl.BlockSpec((D, tn), lambda i, j: (0, j))],
        out_specs=pl.BlockSpec((tm, tn), lambda i, j: (i, j)),
        out_shape=jax.ShapeDtypeStruct((tokens, P_COLS), bf16),
        scratch_shapes=[pltpu.VMEM((tm, D), bf16)],
        compiler_params=pltpu.CompilerParams(
            dimension_semantics=("parallel", "arbitrary"), vmem_limit_bytes=VMEM_LIMIT),
        name="norm_inproj",
    )(x2d, scale, shift, gain, w)


def _rope_kernel(pos_ref, invf_ref, cos_ref, sin_ref):
    ang = pos_ref[...].astype(f32) * invf_ref[...]
    cos_ref[...] = jnp.cos(ang)
    sin_ref[...] = jnp.sin(ang)


def _rope_table(positions, tokens):
    half = MLA_ROPE // 2
    pack = LANES // half
    rows = tokens // pack
    tr = min(2048, rows)
    inv_freq = ROPE_BASE ** (-jnp.arange(0, MLA_ROPE, 2, dtype=f32) / MLA_ROPE)
    invf = jnp.tile(inv_freq, pack).reshape(1, LANES)
    pos = jnp.repeat(positions.reshape(rows, pack), half, axis=1)
    cos, sin = pl.pallas_call(
        _rope_kernel,
        grid=(rows // tr,),
        in_specs=[pl.BlockSpec((tr, LANES), lambda i: (i, 0)),
                  pl.BlockSpec((1, LANES), lambda i: (0, 0))],
        out_specs=[pl.BlockSpec((tr, LANES), lambda i: (i, 0))] * 2,
        out_shape=[jax.ShapeDtypeStruct((rows, LANES), f32)] * 2,
        name="rope_table",
    )(pos, invf)
    cos, sin = cos.reshape(tokens, half), sin.reshape(tokens, half)
    return jnp.concatenate([cos, cos, sin, sin], axis=1)


def _mla_prep_kernel(p_ref, rope_ref, qn_ref, kvn_ref, wq_ref, wkv_ref, q_ref, k_ref, v_ref):
    p = p_ref[...].astype(f32)
    tm = p.shape[0]
    tv = v_ref.shape[-1]
    cq = _rms(p[:, :OFF_CKV], qn_ref[...]).astype(bf16)
    ckv = _rms(p[:, OFF_CKV:OFF_SLAB], kvn_ref[...]).astype(bf16)
    slab = p[:, OFF_SLAB:OFF_SLAB + LANES]
    cos_sin = rope_ref[...]
    lane = lax.broadcasted_iota(jnp.int32, (tm, LANES), 1)
    half = MLA_ROPE // 2
    rot = jnp.where(lane < half, -pltpu.roll(slab, LANES - half, 1), pltpu.roll(slab, half, 1))
    in_rope = lane < MLA_ROPE
    k_rope = slab * cos_sin + rot * pltpu.roll(cos_sin, MLA_ROPE, 1)
    k_rope = jnp.where(in_rope, k_rope, 0.0).astype(bf16)
    scale = (MLA_NOPE + MLA_ROPE) ** -0.5 * LOG2_E
    for h in range(MLA_HEADS):
        rq = jnp.dot(cq, wq_ref[h], preferred_element_type=f32)
        a = rq[:, MLA_NOPE:] * cos_sin
        q_rope = a + pltpu.roll(a, MLA_ROPE, 1)
        q_ref[0, h, :, :MLA_NOPE] = (rq[:, :MLA_NOPE] * scale).astype(bf16)
        q_ref[0, h, :, MLA_NOPE:] = jnp.where(in_rope, q_rope * scale, 0.0).astype(bf16)
        rkv = jnp.dot(ckv, wkv_ref[h], preferred_element_type=f32)
        k_ref[0, h, :, :MLA_NOPE] = rkv[:, :MLA_NOPE].astype(bf16)
        k_ref[0, h, :, MLA_NOPE:] = k_rope
        v_t = rkv[:, MLA_NOPE:].T
        for j in range(v_ref.shape[2]):
            v_ref[0, h, j] = v_t[:, j * tv:(j + 1) * tv].astype(bf16)


def _mla_prep(p, rope, q_norm, kv_norm, wq, wkv, batch, seq):
    tm = min(512, seq)
    tv = _flash_tile(seq)
    per_b = seq // tm
    dqk = MLA_QK_PAD
    const2 = lambda b, i: (0, 0)
    const3 = lambda b, i: (0, 0, 0)
    return pl.pallas_call(
        _mla_prep_kernel,
        grid=(batch, per_b),
        in_specs=[pl.BlockSpec((tm, D), lambda b, i: (b * per_b + i, T_SMALL)),
                  pl.BlockSpec((tm, LANES), lambda b, i: (b * per_b + i, 0)),
                  pl.BlockSpec((1, OFF_CKV), const2),
                  pl.BlockSpec((1, OFF_SLAB - OFF_CKV), const2),
                  pl.BlockSpec(wq.shape, const3),
                  pl.BlockSpec(wkv.shape, const3)],
        out_specs=[pl.BlockSpec((1, MLA_HEADS, tm, dqk), lambda b, i: (b, 0, i, 0)),
                   pl.BlockSpec((1, MLA_HEADS, tm, dqk), lambda b, i: (b, 0, i, 0)),
                   pl.BlockSpec((1, MLA_HEADS, tm // tv, MLA_V, tv),
                                lambda b, i: (b, 0, i, 0, 0))],
        out_shape=[jax.ShapeDtypeStruct((batch, MLA_HEADS, seq, dqk), bf16),
                   jax.ShapeDtypeStruct((batch, MLA_HEADS, seq, dqk), bf16),
                   jax.ShapeDtypeStruct((batch, MLA_HEADS, seq // tv, MLA_V, tv), bf16)],
        compiler_params=pltpu.CompilerParams(
            dimension_semantics=("parallel", "parallel"), vmem_limit_bytes=VMEM_LIMIT),
        name="mla_prep",
    )(p, rope, q_norm, kv_norm, wq, wkv)


def _flash_tile(seq):
    return min(512, seq // 2)


def _flash_kernel(q_ref, k_ref, v_ref, z_ref, o_ref, m_ref, l_ref, acc_ref, *, th):
    qi = pl.program_id(2)
    m_ref[...] = jnp.full(m_ref.shape, -jnp.inf, f32)
    l_ref[...] = jnp.zeros(l_ref.shape, f32)
    acc_ref[...] = jnp.zeros(acc_ref.shape, f32)
    groups = th // SUB

    def run(chains):
        slot = [2 * h + half for h, half, _, _ in chains]
        s = [lax.dot_general(k_ref[0, h, pl.ds(pl.multiple_of(ki * th, th), th), :],
                             q_ref[0, h, half * th:(half + 1) * th, :], NT_DIMS,
                             preferred_element_type=f32)
             for h, half, ki, _ in chains]
        if any(diag for _, _, _, diag in chains):
            k_chunk = _div_pow2(lax.broadcasted_iota(jnp.int32, (th, th), 0), CHUNK)
            q_chunk = _div_pow2(lax.broadcasted_iota(jnp.int32, (th, th), 1), CHUNK)
            allowed = k_chunk <= q_chunk
            s = [jnp.where(allowed, x, -jnp.inf) if diag else x
                 for x, (_, _, _, diag) in zip(s, chains)]
        s = [x.reshape(groups, SUB, th) for x in s]
        m_prev = [m_ref[i] for i in slot]
        m_new = [jnp.maximum(mp, jnp.max(jnp.max(x, axis=0), axis=0, keepdims=True))
                 for mp, x in zip(m_prev, s)]
        alpha = [jnp.exp2(mp - mn) for mp, mn in zip(m_prev, m_new)]
        prob = [jnp.exp2(x - mn[None]) for x, mn in zip(s, m_new)]
        for i, al, pr in zip(slot, alpha, prob):
            l_ref[i] = al * l_ref[i] + jnp.sum(pr, axis=0)
        pv = [jnp.dot(v_ref[0, h, ki], pr.reshape(th, th).astype(bf16),
                      preferred_element_type=f32)
              for pr, (h, _, ki, _) in zip(prob, chains)]
        for i, al, x, mn in zip(slot, alpha, pv, m_new):
            acc = acc_ref[i].reshape(MLA_V // SUB, SUB, th) * al[None]
            acc_ref[i] = acc.reshape(MLA_V, th) + x
            m_ref[i] = mn

    heads = range(FLASH_HEADS)
    n_full = 2 * qi

    def body(pair_idx, carry):
        for ki in (2 * pair_idx, 2 * pair_idx + 1):
            run([(h, half, ki, False) for h in heads for half in range(2)])
        return carry

    lax.fori_loop(0, qi, body, 0)
    run([(h, half, n_full, half == 0) for h in heads for half in range(2)])
    run([(h, 1, n_full + 1, True) for h in heads])
    for h in heads:
        lanes = slice(h * MLA_V, (h + 1) * MLA_V)
        for half in range(2):
            rows = slice(half * th, (half + 1) * th)
            i = 2 * h + half
            inv_l = 1.0 / jnp.sum(l_ref[i], axis=0, keepdims=True)
            o = (acc_ref[i] * inv_l).T
            o_ref[0, rows, lanes] = (o * _silu(z_ref[rows, lanes].astype(f32))).astype(bf16)


def _flash(q, k, v_t, p, batch, seq):
    th = _flash_tile(seq)
    tq = 2 * th
    nq = seq // tq
    dqk = MLA_QK_PAD
    hv = FLASH_HEADS * MLA_V
    z_col0 = T_MLA_Z * (D // hv)
    n_chain = 2 * FLASH_HEADS
    return pl.pallas_call(
        functools.partial(_flash_kernel, th=th),
        grid=(batch, MLA_HEADS // FLASH_HEADS, nq),
        in_specs=[pl.BlockSpec((1, FLASH_HEADS, tq, dqk), lambda b, h, i: (b, h, i, 0)),
                  pl.BlockSpec((1, FLASH_HEADS, seq, dqk), lambda b, h, i: (b, h, 0, 0)),
                  pl.BlockSpec((1, FLASH_HEADS, seq // th, MLA_V, th),
                               lambda b, h, i: (b, h, 0, 0, 0)),
                  pl.BlockSpec((tq, hv), lambda b, h, i: (b * nq + i, z_col0 + h))],
        out_specs=pl.BlockSpec((1, tq, hv), lambda b, h, i: (b, i, h)),
        out_shape=jax.ShapeDtypeStruct((batch, seq, MLA_HEADS * MLA_V), bf16),
        scratch_shapes=[pltpu.VMEM((n_chain, SUB, th), f32),
                        pltpu.VMEM((n_chain, SUB, th), f32),
                        pltpu.VMEM((n_chain, MLA_V, th), f32)],
        compiler_params=pltpu.CompilerParams(
            dimension_semantics=("parallel", "parallel", "arbitrary"),
            vmem_limit_bytes=VMEM_LIMIT),
        name="mla_flash",
    )(q, k, v_t, p)


def _shift_mix(s, prev_rows, mu):
    rolled = pltpu.roll(s, 1, 0)
    sub = SUB
    first = lax.broadcasted_iota(jnp.int32, (sub, s.shape[1]), 0) == 0
    pieces = []
    for i, prev_row in enumerate(prev_rows):
        top = i * CHUNK
        pieces += [jnp.where(first, prev_row, rolled[top:top + sub]),
                   rolled[top + sub:top + CHUNK]]
    prev = jnp.concatenate(pieces, axis=0)
    return s + mu * (prev - s)


def _rwkv_kernel(r_ref, k_ref, v_ref, lora_ref, z_ref, mu_ref, mul_ref, w0_ref, a0_ref, wl_ref,
                 kk_ref, ka_ref, rk_ref, lnw_ref, lnb_ref, seg_ref, ltri_ref, eye_ref, hm_ref,
                 o_ref, state_ref, prev_ref, prevl_ref):
    c_len = CHUNK
    halves = range(PAIR)

    @pl.when(pl.program_id(1) == 0)
    def _():
        state_ref[...] = jnp.zeros(state_ref.shape, f32)
        prev_ref[...] = jnp.zeros(prev_ref.shape, f32)
        prevl_ref[...] = jnp.zeros(prevl_ref.shape, f32)

    def load(ref):
        x = ref[...].astype(f32)
        return x.reshape(PAIR * c_len, x.shape[-1])

    def last_rows(x):
        return [x[(i + 1) * c_len - 1:(i + 1) * c_len, :] for i in halves]

    streams = [load(ref) for ref in (r_ref, k_ref, v_ref)]
    lora_raw = load(lora_ref)
    r, k, v = (_shift_mix(x, [prev_ref[i, j:j + 1, :] for i in halves], mu_ref[j:j + 1, :])
               for j, x in enumerate(streams))
    lora = _shift_mix(lora_raw, [prevl_ref[i, 0:1, :] for i in halves], mul_ref[...])
    for j, x in enumerate(streams):
        for i, last in enumerate(last_rows(x)):
            prev_ref[i, j:j + 1, :] = last
    for i, last in enumerate(last_rows(lora_raw)):
        prevl_ref[i, 0:1, :] = last

    lane = lax.broadcasted_iota(jnp.int32, lora.shape, 1)
    lora = jnp.where(lane < LANES // 2, jnp.tanh(lora), lora)
    proj = jnp.dot(lora.astype(bf16), wl_ref[...], preferred_element_type=f32)
    z = w0_ref[...] + proj[:, :D]
    w_log2 = ((jnp.minimum(z, 0.0) - 0.5) * LOG2_E
              - jnp.log2(1.0 + jnp.exp2(jnp.abs(z) * -LOG2_E)))
    log2_decay = jnp.exp2(w_log2) * -LOG2_E
    a = _sigmoid(a0_ref[...] + proj[:, D:])
    kk = k * kk_ref[...]
    ka = ka_ref[...]
    k = k * (a * ka + (1.0 - ka))

    cum = _chunk_cumsum(ltri_ref[...], log2_decay)
    g_inc = jnp.exp2(cum)
    g_inv = jnp.exp2(-cum)
    g_exc = jnp.exp2(cum - log2_decay)
    g_end = [jnp.exp2(last) for last in last_rows(cum)]

    seg = seg_ref[...]
    row = lax.broadcasted_iota(jnp.int32, (c_len, QUAD), 0)
    col = lax.broadcasted_iota(jnp.int32, (c_len, QUAD), 1)
    col_t = jnp.bitwise_and(col, c_len - 1)
    strict = col_t < row
    incl = col_t <= row
    eye = eye_ref[...]
    head_masks = [hm_ref[g] for g in range(QUAD // RWKV_HEAD)]
    block_diag = (_div_pow2(lax.broadcasted_iota(jnp.int32, (QUAD, QUAD), 0), RWKV_HEAD)
                  == _div_pow2(lax.broadcasted_iota(jnp.int32, (QUAD, QUAD), 1), RWKV_HEAD))

    def stack(y):
        y = y.astype(bf16)
        return jnp.concatenate([y * m for m in head_masks], axis=0)

    def bdot(x, y, dims=NN_DIMS):
        return lax.dot_general(x.astype(bf16), y.astype(bf16), dims, preferred_element_type=f32)

    def qprod(x, y):
        return bdot(x, stack(y))

    slabs = [slice(qd * QUAD, (qd + 1) * QUAD) for qd in range(N_QUADS)]
    items = [(i, qd) for i in halves for qd in range(N_QUADS)]
    each = lambda fn, *cols: [fn(*args) for args in zip(*cols)]
    cut = lambda x: [x[i * c_len:(i + 1) * c_len, slabs[qd]] for i, qd in items]

    kk_q = cut(kk)
    kk_sq = _seg_sums([x * x for x in kk_q], seg)
    kk_q = each(lambda x, sq: x * lax.rsqrt(jnp.maximum(sq, 1e-24)), kk_q, kk_sq)
    a_bar = each(lambda x, g: -x * g, kk_q, cut(g_exc))
    b_til = each(lambda x, a_q, g: x * a_q * g, kk_q, cut(a), cut(g_inv))
    k_til = each(jnp.multiply, cut(k), cut(g_inv))
    r_bar = each(jnp.multiply, cut(r), cut(g_inc))
    g_end_q = [g_end[i][:, slabs[qd]] for i, qd in items]
    b_end = each(jnp.multiply, b_til, g_end_q)
    k_end = each(jnp.multiply, k_til, g_end_q)
    v_q = cut(v)

    pair = each(lambda ab, rb, bt, kt: bdot(jnp.concatenate([ab, rb], axis=0),
                                            jnp.concatenate([stack(bt), stack(kt)], axis=0),
                                            NT_DIMS),
                a_bar, r_bar, b_til, k_til)
    a_ab = [jnp.where(strict, x[:c_len, :QUAD], 0.0) for x in pair]
    a_ak = [jnp.where(strict, x[:c_len, QUAD:], 0.0) for x in pair]
    a_rb = [jnp.where(incl, x[c_len:, :QUAD], 0.0) for x in pair]
    a_rk = [jnp.where(incl, x[c_len:, QUAD:], 0.0) for x in pair]

    t_inv = [eye + x for x in a_ab]
    power = each(qprod, a_ab, a_ab)
    span = 4
    while span < c_len:
        both = each(lambda t, pw: qprod(jnp.concatenate([t, pw], axis=0), pw), t_inv, power)
        t_inv = each(lambda t, x: t + x[:c_len], t_inv, both)
        power = [x[c_len:] for x in both]
        span *= 2
    t_inv = each(lambda t, pw: t + qprod(t, pw), t_inv, power)

    akv = each(lambda aak, ark, vq: qprod(jnp.concatenate([aak, ark], axis=0), vq),
               a_ak, a_rk, v_q)
    wu = each(lambda t, ab, x: bdot(t, jnp.concatenate([stack(ab), stack(x[:c_len])], axis=1)),
              t_inv, a_bar, akv)
    state = [state_ref[i, qd] for i, qd in items]
    on_state = each(lambda x, rb, st: bdot(jnp.concatenate([x[:, :QUAD], rb], axis=0), st,
                                           NT_DIMS), wu, r_bar, state)
    u = each(lambda hs, x: hs[:c_len] + x[:, QUAD:], on_state, wu)
    y = each(lambda hs, arb, u_q, x: hs[c_len:] + qprod(arb, u_q) + x[c_len:],
             on_state, a_rb, u, akv)
    upd = each(lambda u_q, vq, be, ke: bdot(jnp.concatenate([u_q, vq], axis=0),
                                            jnp.concatenate([be, ke], axis=0), TN_DIMS),
               u, v_q, b_end, k_end)
    for n, (i, qd) in enumerate(items):
        state_ref[i, qd] = state[n] * g_end_q[n] + jnp.where(block_diag, upd[n], 0.0)

    inv_n = 1.0 / RWKV_HEAD
    rkr = cut(r * k * rk_ref[...])
    sums = _seg_sums(y + rkr, seg)
    dev = each(lambda y_q, total: y_q - total * inv_n, y, sums[:len(items)])
    bonus = each(jnp.multiply, sums[len(items):], v_q)
    var = [total * inv_n for total in _seg_sums([d_q * d_q for d_q in dev], seg)]
    for n, (i, qd) in enumerate(items):
        sl = slabs[qd]
        y_q = dev[n] * lax.rsqrt(var[n] + RWKV_GN_EPS) * lnw_ref[:, sl] + lnb_ref[:, sl]
        o_ref[i, :, sl] = ((y_q + bonus[n]) * _silu(z_ref[i, :, sl].astype(f32))).astype(bf16)


def _rwkv(p, mu_rkv, mu_lora, w0, a0, w_lora, k_k, k_a, r_k, ln_w, ln_b, seg, ltri, batch, seq):
    nc = seq // CHUNK
    col = jnp.arange(QUAD)
    eye = (col[None, :] % CHUNK == jnp.arange(CHUNK)[:, None]).astype(f32)
    head_masks = jnp.broadcast_to(
        (col[None, None, :] // RWKV_HEAD == jnp.arange(QUAD // RWKV_HEAD)[:, None, None]),
        (QUAD // RWKV_HEAD, CHUNK, QUAD)).astype(bf16)
    lora_col = T_SMALL * (D // LANES) + OFF_LORA // LANES
    tile = lambda t: pl.BlockSpec((PAIR, CHUNK, D), lambda b, c: (b, c, t))
    const = lambda shape: pl.BlockSpec(shape, lambda b, c: (0, 0))
    return pl.pallas_call(
        _rwkv_kernel,
        grid=(batch // PAIR, nc),
        in_specs=[tile(T_R), tile(T_K), tile(T_V),
                  pl.BlockSpec((PAIR, CHUNK, LANES), lambda b, c: (b, c, lora_col)),
                  tile(T_RZ),
                  const(mu_rkv.shape), const(mu_lora.shape), const((1, D)), const((1, D)),
                  const(w_lora.shape),
                  const((1, D)), const((1, D)), const((1, D)), const((1, D)), const((1, D)),
                  const(seg.shape), const(ltri.shape), const(eye.shape),
                  pl.BlockSpec(head_masks.shape, lambda b, c: (0, 0, 0))],
        out_specs=pl.BlockSpec((PAIR, CHUNK, D), lambda b, c: (b, c, 0)),
        out_shape=jax.ShapeDtypeStruct((batch, seq, D), bf16),
        scratch_shapes=[pltpu.VMEM((PAIR, N_QUADS, QUAD, QUAD), f32),
                        pltpu.VMEM((PAIR, 8, D), f32), pltpu.VMEM((PAIR, 8, LANES), f32)],
        compiler_params=pltpu.CompilerParams(
            dimension_semantics=("parallel", "arbitrary"), vmem_limit_bytes=VMEM_LIMIT),
        name="rwkv7_chunk",
    )(p, p, p, p, p, mu_rkv, mu_lora, w0, a0, w_lora, k_k, k_a, r_k, ln_w, ln_b, seg, ltri, eye,
      head_masks)


def _gla_kernel(qk_ref, v_ref, gl_ref, z_ref, w2_ref, gb_ref, ng_ref, ltri_ref, o_ref, state_ref):
    c_len = CHUNK
    kw = GLA_HEADS * GLA_DK
    halves = range(PAIR)

    @pl.when(pl.program_id(1) == 0)
    def _():
        state_ref[...] = jnp.zeros(state_ref.shape, f32)

    qk = qk_ref[...].astype(f32).reshape(PAIR * c_len, 2 * kw)
    q = qk[:, :kw] * (GLA_DK ** -0.5)
    k = qk[:, kw:]
    gl = gl_ref[...].reshape(PAIR * c_len, LANES)
    gate = jnp.dot(gl, w2_ref[...], preferred_element_type=f32) + gb_ref[...]
    log_a = -_softplus(-gate) / GLA_GATE_NORM
    cum = _chunk_cumsum(ltri_ref[...], log_a)
    rows = [slice(i * c_len, (i + 1) * c_len) for i in halves]
    cum_last = [cum[sl.stop - 1:sl.stop, :] for sl in rows]
    q_dec = (q * jnp.exp(cum)).astype(bf16)
    k_inv = (k * jnp.exp(-cum)).astype(bf16)
    k_end = [(k[sl] * jnp.exp(last - cum[sl])).astype(bf16) for sl, last in zip(rows, cum_last)]
    decay = [jnp.exp(last) for last in cum_last]
    causal = (lax.broadcasted_iota(jnp.int32, (c_len, c_len), 1)
              <= lax.broadcasted_iota(jnp.int32, (c_len, c_len), 0))
    ks = [slice(h * GLA_DK, (h + 1) * GLA_DK) for h in range(GLA_HEADS)]
    vs = [slice(h * GLA_DV, (h + 1) * GLA_DV) for h in range(GLA_HEADS)]
    items = [(i, h) for i in halves for h in range(GLA_HEADS)]
    v_h = [v_ref[i, :, vs[h]] for i, h in items]
    state = [state_ref[i, h] for i, h in items]
    attn = [lax.dot_general(q_dec[rows[i], ks[h]], k_inv[rows[i], ks[h]], NT_DIMS,
                            preferred_element_type=f32) for i, h in items]
    attn = [jnp.where(causal, x, 0.0).astype(bf16) for x in attn]
    o = [jnp.dot(attn[n], v_h[n], preferred_element_type=f32)
         + lax.dot_general(q_dec[rows[i], ks[h]], state[n].astype(bf16), NT_DIMS,
                           preferred_element_type=f32) for n, (i, h) in enumerate(items)]
    for n, (i, h) in enumerate(items):
        state_ref[i, h] = state[n] * decay[i][:, ks[h]] + lax.dot_general(
            v_h[n], k_end[i][:, ks[h]], TN_DIMS, preferred_element_type=f32)
    for n, (i, h) in enumerate(items):
        o_h = o[n] * lax.rsqrt(jnp.mean(o[n] * o[n], axis=-1, keepdims=True) + NORM_EPS)
        gate_z = _silu(z_ref[i, :, vs[h]].astype(f32))
        o_ref[i, :, vs[h]] = (o_h * ng_ref[...] * gate_z).astype(bf16)


def _gla(p, w2_pad, g_bias, norm_g, ltri, batch, seq):
    nc = seq // CHUNK
    slab_col = T_SMALL * (D // LANES) + OFF_SLAB // LANES
    tile = lambda t: pl.BlockSpec((PAIR, CHUNK, D), lambda b, c: (b, c, t))
    const = lambda shape: pl.BlockSpec(shape, lambda b, c: (0, 0))
    return pl.pallas_call(
        _gla_kernel,
        grid=(batch // PAIR, nc),
        in_specs=[tile(T_GQK), tile(T_GV),
                  pl.BlockSpec((PAIR, CHUNK, LANES), lambda b, c: (b, c, slab_col)),
                  tile(T_GZ),
                  const(w2_pad.shape), const(g_bias.shape), const(norm_g.shape),
                  const(ltri.shape)],
        out_specs=pl.BlockSpec((PAIR, CHUNK, D), lambda b, c: (b, c, 0)),
        out_shape=jax.ShapeDtypeStruct((batch, seq, D), bf16),
        scratch_shapes=[pltpu.VMEM((PAIR, GLA_HEADS, GLA_DV, GLA_DK), f32)],
        compiler_params=pltpu.CompilerParams(
            dimension_semantics=("parallel", "arbitrary"), vmem_limit_bytes=VMEM_LIMIT),
        name="gla_chunk",
    )(p, p, p, p, w2_pad, g_bias, norm_g, ltri)


def _merge_kernel(om_ref, or_ref, og_ref, ga_ref, gb_ref, gc_ref, x_ref, gate_ref, np_ref,
                  wm_ref, wr_ref, wg_ref, wo_ref, o_ref):
    gate_of = lambda ref: _sigmoid(ref[...].astype(f32))
    merged = (gate_of(ga_ref) * jnp.dot(om_ref[...], wm_ref[...], preferred_element_type=f32)
              + gate_of(gb_ref) * jnp.dot(or_ref[...], wr_ref[...], preferred_element_type=f32)
              + gate_of(gc_ref) * jnp.dot(og_ref[...], wg_ref[...], preferred_element_type=f32))
    y = jnp.dot(merged.astype(bf16), wo_ref[...], preferred_element_type=f32)
    o_ref[...] = x_ref[...] + gate_ref[0] * _rms(y, np_ref[...])


def _merge(o_mla, o_rwkv, o_gla, p, x2d, gate, norm_post, w_mla, w_rwkv, w_gla, w_out, seq):
    tokens = x2d.shape[0]
    tm = min(512, seq)
    per_b = seq // tm
    row = lambda i: (i, 0)
    const = lambda i: (0, 0)
    return pl.pallas_call(
        _merge_kernel,
        grid=(tokens // tm,),
        in_specs=[pl.BlockSpec((tm, D), row), pl.BlockSpec((tm, D), row),
                  pl.BlockSpec((tm, D), row),
                  pl.BlockSpec((tm, D), lambda i: (i, T_GA)),
                  pl.BlockSpec((tm, D), lambda i: (i, T_GB)),
                  pl.BlockSpec((tm, D), lambda i: (i, T_GC)),
                  pl.BlockSpec((tm, D), row),
                  pl.BlockSpec((1, 1, D), lambda i: (i // per_b, 0, 0)),
                  pl.BlockSpec((1, D), const),
                  pl.BlockSpec((D, D), const), pl.BlockSpec((D, D), const),
                  pl.BlockSpec((D, D), const), pl.BlockSpec((D, D), const)],
        out_specs=pl.BlockSpec((tm, D), row),
        out_shape=jax.ShapeDtypeStruct((tokens, D), f32),
        compiler_params=pltpu.CompilerParams(
            dimension_semantics=("parallel",), vmem_limit_bytes=VMEM_LIMIT),
        name="merge_outproj",
    )(o_mla, o_rwkv, o_gla, p, p, p, x2d, gate, norm_post, w_mla, w_rwkv, w_gla, w_out)


def _rot_half_cols(w):
    half = w.shape[-1] // 2
    return jnp.concatenate([-w[..., half:], w[..., :half]], axis=-1)


def _permute_w_in(w):
    sizes = (512, 256, 64, 1024, 1024, 1024, 1024, 64, 64, 1024, 512, 512, 1024, 16, 1024,
             1024, 1024, 1024)
    parts = []
    start = 0
    for size in sizes:
        parts.append(w[:, start:start + size])
        start += size
    (c_q, c_kv, k_rope, mla_z, r, k, v, w_l, a_l, r_z, g_q, g_k, g_v, g_l, g_z,
     gate_a, gate_b, gate_c) = parts
    pad = jnp.zeros((w.shape[0], OFF_LORA - OFF_SLAB - MLA_ROPE - g_l.shape[1]), w.dtype)
    return jnp.concatenate([c_q, c_kv, k_rope, g_l, pad, w_l, a_l, mla_z, r, k, v, r_z,
                            g_q, g_k, g_v, g_z, gate_a, gate_b, gate_c], axis=1).astype(bf16)


def _mla_weights(w_uq, w_ukv):
    n_in = w_uq.shape[0]
    wq = w_uq.reshape(n_in, MLA_HEADS, MLA_NOPE + MLA_ROPE)
    wq = jnp.concatenate([wq, _rot_half_cols(wq[..., MLA_NOPE:])], axis=-1)
    wkv = w_ukv.reshape(w_ukv.shape[0], MLA_HEADS, MLA_NOPE + MLA_V)
    return wq.transpose(1, 0, 2).astype(bf16), wkv.transpose(1, 0, 2).astype(bf16)


def kernel(x, c, positions, ada_w, ada_b, norm_pre, norm_post, w_in, rwkv_mu, mla_q_norm,
           mla_kv_norm, mla_w_uq, mla_w_ukv, mla_w_o, rwkv_w0, rwkv_w2, rwkv_a0, rwkv_a2,
           rwkv_k_k, rwkv_k_a, rwkv_r_k, rwkv_ln_w, rwkv_ln_b, rwkv_w_o, gla_w2, gla_b,
           gla_norm, gla_w_o, w_out):
    batch, seq, _ = x.shape
    assert batch % PAIR == 0 and seq % (2 * CHUNK) == 0, (batch, seq)
    depth = w_in.shape[0]
    tokens = batch * seq
    x2d = x.reshape(tokens, D)
    rope = _rope_table(positions, tokens)
    idx = jnp.arange(SEG_WIDTH)
    seg = (idx[:, None] // RWKV_HEAD == idx[None, :] // RWKV_HEAD).astype(bf16)
    tri = jnp.arange(min(PAIR * CHUNK, SEG_WIDTH))
    ltri = ((tri[None, :] <= tri[:, None])
            & (tri[None, :] // CHUNK == tri[:, None] // CHUNK)).astype(bf16)
    row1 = lambda t: t.reshape(1, -1)

    mod = _modulation(c, ada_w, ada_b)
    for l in range(depth):
        shift = mod[l, :, :D].reshape(batch, 1, D)
        scale = mod[l, :, D:2 * D].reshape(batch, 1, D)
        gate = mod[l, :, 2 * D:].reshape(batch, 1, D)
        p = _inproj(x2d, scale, shift, row1(norm_pre[l]), _permute_w_in(w_in[l]), seq)

        wq, wkv = _mla_weights(mla_w_uq[l], mla_w_ukv[l])
        q, k, v = _mla_prep(p, rope, row1(mla_q_norm[l]), row1(mla_kv_norm[l]), wq, wkv,
                            batch, seq)
        o_mla = _flash(q, k, v, p, batch, seq).reshape(tokens, D)

        mu = rwkv_mu[l]
        mu_rkv = jnp.zeros((8, D), f32).at[:3].set(mu[:3 * D].reshape(3, D))
        w_lora = jnp.zeros((LANES, 2 * D), f32)
        w_lora = w_lora.at[:LANES // 2, :D].set(rwkv_w2[l]).at[LANES // 2:, D:].set(rwkv_a2[l])
        p3 = p.reshape(batch, seq, P_COLS)
        o_rwkv = _rwkv(p3, mu_rkv, row1(mu[3 * D:]), row1(rwkv_w0[l]), row1(rwkv_a0[l]),
                       w_lora.astype(bf16), row1(rwkv_k_k[l]), row1(rwkv_k_a[l]),
                       row1(rwkv_r_k[l]), row1(rwkv_ln_w[l]), row1(rwkv_ln_b[l]), seg, ltri,
                       batch, seq).reshape(tokens, D)

        w2_pad = jnp.zeros((LANES, GLA_HEADS * GLA_DK), f32)
        w2_pad = w2_pad.at[OFF_GL:OFF_GL + gla_w2.shape[1]].set(gla_w2[l]).astype(bf16)
        o_gla = _gla(p3, w2_pad, row1(gla_b[l]), row1(gla_norm[l]), ltri, batch,
                     seq).reshape(tokens, D)

        x2d = _merge(o_mla, o_rwkv, o_gla, p, x2d, gate, row1(norm_post[l]),
                     mla_w_o[l].astype(bf16), rwkv_w_o[l].astype(bf16),
                     gla_w_o[l].astype(bf16), w_out[l].astype(bf16), seq)
    return x2d.reshape(batch, seq, D)
```

```python
import functools

import jax
import jax.numpy as jnp
from jax import lax
from jax.experimental import pallas as pl
from jax.experimental.pallas import tpu as pltpu

f32 = jnp.float32
bf16 = jnp.bfloat16

D = 1024
CHUNK = 64
PAIR = 8
NORM_EPS = 1e-6
MLA_HEADS = 8
MLA_NOPE = 128
MLA_ROPE = 64
MLA_V = 128
MLA_QK_PAD = 256
FLASH_HEADS = 4
ROPE_BASE = 10000.0
LOG2_E = 1.4426950408889634
RWKV_HEAD = 64
RWKV_GN_EPS = 64e-5
GLA_HEADS = 4
GLA_DK = 128
GLA_DV = 256
GLA_GATE_NORM = 16.0
LANES = 128
SUB = 8
QUAD = 128
N_QUADS = D // QUAD
SEG_WIDTH = 256
P_COLS = 12 * D
VMEM_LIMIT = 56 * 1024 * 1024

T_SMALL, T_MLA_Z, T_R, T_K, T_V, T_RZ, T_GQK, T_GV, T_GZ, T_GA, T_GB, T_GC = range(12)
OFF_CKV = 512
OFF_SLAB = 768
OFF_GL = 64
OFF_LORA = 896

NN_DIMS = (((1,), (0,)), ((), ()))
NT_DIMS = (((1,), (1,)), ((), ()))
TN_DIMS = (((0,), (0,)), ((), ()))


def _div_pow2(x, n):
    return lax.shift_right_logical(x, jnp.int32(n.bit_length() - 1))


def _softplus(x):
    return jnp.maximum(x, 0.0) + jnp.log(1.0 + jnp.exp(-jnp.abs(x)))


def _sigmoid(x):
    return 1.0 / (1.0 + jnp.exp(-x))


def _silu(x):
    return x * _sigmoid(x)


def _split_bf16(x):
    hi = x.astype(bf16)
    return hi, (x - hi.astype(f32)).astype(bf16)


def _seg_sums(xs, seg):
    rows, width = xs[0].shape
    join = seg.shape[0] // width
    wide = [jnp.concatenate(xs[i:i + join], axis=1) for i in range(0, len(xs), join)]
    out = jnp.dot(jnp.concatenate(wide, axis=0).astype(bf16), seg, preferred_element_type=f32)
    return [out[(i // join) * rows:(i // join + 1) * rows,
                (i % join) * width:(i % join + 1) * width] for i in range(len(xs))]


def _chunk_cumsum(ltri, x):
    hi, lo = _split_bf16(x)
    size = ltri.shape[0]
    groups = [slice(g, g + size) for g in range(0, x.shape[0], size)]
    return jnp.concatenate(
        [jnp.dot(ltri, hi[sl], preferred_element_type=f32)
         + jnp.dot(ltri, lo[sl], preferred_element_type=f32) for sl in groups], axis=0)


def _rms(x, gain):
    return x * lax.rsqrt(jnp.mean(x * x, axis=-1, keepdims=True) + NORM_EPS) * gain


def _mod_kernel(c_ref, w_ref, b_ref, o_ref):
    c = _silu(c_ref[...]).astype(bf16)
    o_ref[0] = jnp.dot(c, w_ref[0].astype(bf16), preferred_element_type=f32) + b_ref[0]


def _modulation(c, ada_w, ada_b):
    depth, _, n3 = ada_w.shape
    batch = c.shape[0]
    return pl.pallas_call(
        _mod_kernel,
        grid=(depth, n3 // D),
        in_specs=[pl.BlockSpec((batch, D), lambda l, j: (0, 0)),
                  pl.BlockSpec((1, D, D), lambda l, j: (l, 0, j)),
                  pl.BlockSpec((1, 1, D), lambda l, j: (l, 0, j))],
        out_specs=pl.BlockSpec((1, batch, D), lambda l, j: (l, 0, j)),
        out_shape=jax.ShapeDtypeStruct((depth, batch, n3), f32),
        name="adaln_mod",
    )(c, ada_w, ada_b.reshape(depth, 1, n3))


def _inproj_kernel(x_ref, scale_ref, shift_ref, g_ref, w_ref, o_ref, h_ref):
    @pl.when(pl.program_id(1) == 0)
    def _():
        h = _rms(x_ref[...], g_ref[...]) * (1.0 + scale_ref[0]) + shift_ref[0]
        h_ref[...] = h.astype(bf16)

    o_ref[...] = jnp.dot(h_ref[...], w_ref[...], preferred_element_type=f32).astype(bf16)


def _inproj(x2d, scale, shift, gain, w, seq):
    tokens = x2d.shape[0]
    tm = min(2048, seq)
    tn = 2 * D
    per_b = seq // tm
    return pl.pallas_call(
        _inproj_kernel,
        grid=(tokens // tm, P_COLS // tn),
        in_specs=[pl.BlockSpec((tm, D), lambda i, j: (i, 0)),
                  pl.BlockSpec((1, 1, D), lambda i, j: (i // per_b, 0, 0)),
                  pl.BlockSpec((1, 1, D), lambda i, j: (i // per_b, 0, 0)),
                  pl.BlockSpec((1, D), lambda i, j: (0, 0)),
                  pl.BlockSpec((D, tn), lambda i, j: (0, j))],
        out_specs=pl.BlockSpec((tm, tn), lambda i, j: (i, j)),
        out_shape=jax.ShapeDtypeStruct((tokens, P_COLS), bf16),
        scratch_shapes=[pltpu.VMEM((tm, D), bf16)],
        compiler_params=pltpu.CompilerParams(
            dimension_semantics=("parallel", "arbitrary"), vmem_limit_bytes=VMEM_LIMIT),
        name="norm_inproj",
    )(x2d, scale, shift, gain, w)


def _rope_kernel(pos_ref, invf_ref, cos_ref, sin_ref):
    ang = pos_ref[...].astype(f32) * invf_ref[...]
    cos_ref[...] = jnp.cos(ang)
    sin_ref[...] = jnp.sin(ang)


def _rope_table(positions, tokens):
    half = MLA_ROPE // 2
    pack = LANES // half
    rows = tokens // pack
    tr = min(2048, rows)
    inv_freq = ROPE_BASE ** (-jnp.arange(0, MLA_ROPE, 2, dtype=f32) / MLA_ROPE)
    invf = jnp.tile(inv_freq, pack).reshape(1, LANES)
    pos = jnp.repeat(positions.reshape(rows, pack), half, axis=1)
    cos, sin = pl.pallas_call(
        _rope_kernel,
        grid=(rows // tr,),
        in_specs=[pl.BlockSpec((tr, LANES), lambda i: (i, 0)),
                  pl.BlockSpec((1, LANES), lambda i: (0, 0))],
        out_specs=[pl.BlockSpec((tr, LANES), lambda i: (i, 0))] * 2,
        out_shape=[jax.ShapeDtypeStruct((rows, LANES), f32)] * 2,
        name="rope_table",
    )(pos, invf)
    cos, sin = cos.reshape(tokens, half), sin.reshape(tokens, half)
    return jnp.concatenate([cos, cos, sin, sin], axis=1)


def _mla_prep_kernel(p_ref, rope_ref, qn_ref, kvn_ref, wq_ref, wkv_ref, q_ref, k_ref, v_ref):
    p = p_ref[...].astype(f32)
    tm = p.shape[0]
    tv = v_ref.shape[-1]
    cq = _rms(p[:, :OFF_CKV], qn_ref[...]).astype(bf16)
    ckv = _rms(p[:, OFF_CKV:OFF_SLAB], kvn_ref[...]).astype(bf16)
    slab = p[:, OFF_SLAB:OFF_SLAB + LANES]
    cos_sin = rope_ref[...]
    lane = lax.broadcasted_iota(jnp.int32, (tm, LANES), 1)
    half = MLA_ROPE // 2
    rot = jnp.where(lane < half, -pltpu.roll(slab, LANES - half, 1), pltpu.roll(slab, half, 1))
    in_rope = lane < MLA_ROPE
    k_rope = slab * cos_sin + rot * pltpu.roll(cos_sin, MLA_ROPE, 1)
    k_rope = jnp.where(in_rope, k_rope, 0.0).astype(bf16)
    scale = (MLA_NOPE + MLA_ROPE) ** -0.5 * LOG2_E
    for h in range(MLA_HEADS):
        rq = jnp.dot(cq, wq_ref[h], preferred_element_type=f32)
        a = rq[:, MLA_NOPE:] * cos_sin
        q_rope = a + pltpu.roll(a, MLA_ROPE, 1)
        q_ref[0, h, :, :MLA_NOPE] = (rq[:, :MLA_NOPE] * scale).astype(bf16)
        q_ref[0, h, :, MLA_NOPE:] = jnp.where(in_rope, q_rope * scale, 0.0).astype(bf16)
        rkv = jnp.dot(ckv, wkv_ref[h], preferred_element_type=f32)
        k_ref[0, h, :, :MLA_NOPE] = rkv[:, :MLA_NOPE].astype(bf16)
        k_ref[0, h, :, MLA_NOPE:] = k_rope
        v_t = rkv[:, MLA_NOPE:].T
        for j in range(v_ref.shape[2]):
            v_ref[0, h, j] = v_t[:, j * tv:(j + 1) * tv].astype(bf16)


def _mla_prep(p, rope, q_norm, kv_norm, wq, wkv, batch, seq):
    tm = min(512, seq)
    tv = _flash_tile(seq)
    per_b = seq // tm
    dqk = MLA_QK_PAD
    const2 = lambda b, i: (0, 0)
    const3 = lambda b, i: (0, 0, 0)
    return pl.pallas_call(
        _mla_prep_kernel,
        grid=(batch, per_b),
        in_specs=[pl.BlockSpec((tm, D), lambda b, i: (b * per_b + i, T_SMALL)),
                  pl.BlockSpec((tm, LANES), lambda b, i: (b * per_b + i, 0)),
                  pl.BlockSpec((1, OFF_CKV), const2),
                  pl.BlockSpec((1, OFF_SLAB - OFF_CKV), const2),
                  pl.BlockSpec(wq.shape, const3),
                  pl.BlockSpec(wkv.shape, const3)],
        out_specs=[pl.BlockSpec((1, MLA_HEADS, tm, dqk), lambda b, i: (b, 0, i, 0)),
                   pl.BlockSpec((1, MLA_HEADS, tm, dqk), lambda b, i: (b, 0, i, 0)),
                   pl.BlockSpec((1, MLA_HEADS, tm // tv, MLA_V, tv),
                                lambda b, i: (b, 0, i, 0, 0))],
        out_shape=[jax.ShapeDtypeStruct((batch, MLA_HEADS, seq, dqk), bf16),
                   jax.ShapeDtypeStruct((batch, MLA_HEADS, seq, dqk), bf16),
                   jax.ShapeDtypeStruct((batch, MLA_HEADS, seq // tv, MLA_V, tv), bf16)],
        compiler_params=pltpu.CompilerParams(
            dimension_semantics=("parallel", "parallel"), vmem_limit_bytes=VMEM_LIMIT),
        name="mla_prep",
    )(p, rope, q_norm, kv_norm, wq, wkv)


def _flash_tile(seq):
    return min(512, seq // 2)


def _flash_kernel(q_ref, k_ref, v_ref, z_ref, o_ref, m_ref, l_ref, acc_ref, *, th):
    qi = pl.program_id(2)
    m_ref[...] = jnp.full(m_ref.shape, -jnp.inf, f32)
    l_ref[...] = jnp.zeros(l_ref.shape, f32)
    acc_ref[...] = jnp.zeros(acc_ref.shape, f32)
    groups = th // SUB

    def run(chains):
        slot = [2 * h + half for h, half, _, _ in chains]
        s = [lax.dot_general(k_ref[0, h, pl.ds(pl.multiple_of(ki * th, th), th), :],
                             q_ref[0, h, half * th:(half + 1) * th, :], NT_DIMS,
                             preferred_element_type=f32)
             for h, half, ki, _ in chains]
        if any(diag for _, _, _, diag in chains):
            k_chunk = _div_pow2(lax.broadcasted_iota(jnp.int32, (th, th), 0), CHUNK)
            q_chunk = _div_pow2(lax.broadcasted_iota(jnp.int32, (th, th), 1), CHUNK)
            allowed = k_chunk <= q_chunk
            s = [jnp.where(allowed, x, -jnp.inf) if diag else x
                 for x, (_, _, _, diag) in zip(s, chains)]
        s = [x.reshape(groups, SUB, th) for x in s]
        m_prev = [m_ref[i] for i in slot]
        m_new = [jnp.maximum(mp, jnp.max(jnp.max(x, axis=0), axis=0, keepdims=True))
                 for mp, x in zip(m_prev, s)]
        alpha = [jnp.exp2(mp - mn) for mp, mn in zip(m_prev, m_new)]
        prob = [jnp.exp2(x - mn[None]) for x, mn in zip(s, m_new)]
        for i, al, pr in zip(slot, alpha, prob):
            l_ref[i] = al * l_ref[i] + jnp.sum(pr, axis=0)
        pv = [jnp.dot(v_ref[0, h, ki], pr.reshape(th, th).astype(bf16),
                      preferred_element_type=f32)
              for pr, (h, _, ki, _) in zip(prob, chains)]
        for i, al, x, mn in zip(slot, alpha, pv, m_new):
            acc = acc_ref[i].reshape(MLA_V // SUB, SUB, th) * al[None]
            acc_ref[i] = acc.reshape(MLA_V, th) + x
            m_ref[i] = mn

    heads = range(FLASH_HEADS)
    n_full = 2 * qi

    def body(pair_idx, carry):
        for ki in (2 * pair_idx, 2 * pair_idx + 1):
            run([(h, half, ki, False) for h in heads for half in range(2)])
        return carry

    lax.fori_loop(0, qi, body, 0)
    run([(h, half, n_full, half == 0) for h in heads for half in range(2)])
    run([(h, 1, n_full + 1, True) for h in heads])
    for h in heads:
        lanes = slice(h * MLA_V, (h + 1) * MLA_V)
        for half in range(2):
            rows = slice(half * th, (half + 1) * th)
            i = 2 * h + half
            inv_l = 1.0 / jnp.sum(l_ref[i], axis=0, keepdims=True)
            o = (acc_ref[i] * inv_l).T
            o_ref[0, rows, lanes] = (o * _silu(z_ref[rows, lanes].astype(f32))).astype(bf16)


def _flash(q, k, v_t, p, batch, seq):
    th = _flash_tile(seq)
    tq = 2 * th
    nq = seq // tq
    dqk = MLA_QK_PAD
    hv = FLASH_HEADS * MLA_V
    z_col0 = T_MLA_Z * (D // hv)
    n_chain = 2 * FLASH_HEADS
    return pl.pallas_call(
        functools.partial(_flash_kernel, th=th),
        grid=(batch, MLA_HEADS // FLASH_HEADS, nq),
        in_specs=[pl.BlockSpec((1, FLASH_HEADS, tq, dqk), lambda b, h, i: (b, h, i, 0)),
                  pl.BlockSpec((1, FLASH_HEADS, seq, dqk), lambda b, h, i: (b, h, 0, 0)),
                  pl.BlockSpec((1, FLASH_HEADS, seq // th, MLA_V, th),
                               lambda b, h, i: (b, h, 0, 0, 0)),
                  pl.BlockSpec((tq, hv), lambda b, h, i: (b * nq + i, z_col0 + h))],
        out_specs=pl.BlockSpec((1, tq, hv), lambda b, h, i: (b, i, h)),
        out_shape=jax.ShapeDtypeStruct((batch, seq, MLA_HEADS * MLA_V), bf16),
        scratch_shapes=[pltpu.VMEM((n_chain, SUB, th), f32),
                        pltpu.VMEM((n_chain, SUB, th), f32),
                        pltpu.VMEM((n_chain, MLA_V, th), f32)],
        compiler_params=pltpu.CompilerParams(
            dimension_semantics=("parallel", "parallel", "arbitrary"),
            vmem_limit_bytes=VMEM_LIMIT),
        name="mla_flash",
    )(q, k, v_t, p)


def _shift_mix(s, prev_rows, mu):
    rolled = pltpu.roll(s, 1, 0)
    sub = SUB
    first = lax.broadcasted_iota(jnp.int32, (sub, s.shape[1]), 0) == 0
    pieces = []
    for i, prev_row in enumerate(prev_rows):
        top = i * CHUNK
        pieces += [jnp.where(first, prev_row, rolled[top:top + sub]),
                   rolled[top + sub:top + CHUNK]]
    prev = jnp.concatenate(pieces, axis=0)
    return s + mu * (prev - s)


def _rwkv_kernel(r_ref, k_ref, v_ref, lora_ref, z_ref, mu_ref, mul_ref, w0_ref, a0_ref, wl_ref,
                 kk_ref, ka_ref, rk_ref, lnw_ref, lnb_ref, seg_ref, ltri_ref, eye_ref, hm_ref,
                 o_ref, state_ref, prev_ref, prevl_ref):
    c_len = CHUNK
    halves = range(PAIR)

    @pl.when(pl.program_id(1) == 0)
    def _():
        state_ref[...] = jnp.zeros(state_ref.shape, f32)
        prev_ref[...] = jnp.zeros(prev_ref.shape, f32)
        prevl_ref[...] = jnp.zeros(prevl_ref.shape, f32)

    def load(ref):
        x = ref[...].astype(f32)
        return x.reshape(PAIR * c_len, x.shape[-1])

    def last_rows(x):
        return [x[(i + 1) * c_len - 1:(i + 1) * c_len, :] for i in halves]

    streams = [load(ref) for ref in (r_ref, k_ref, v_ref)]
    lora_raw = load(lora_ref)
    r, k, v = (_shift_mix(x, [prev_ref[i, j:j + 1, :] for i in halves], mu_ref[j:j + 1, :])
               for j, x in enumerate(streams))
    lora = _shift_mix(lora_raw, [prevl_ref[i, 0:1, :] for i in halves], mul_ref[...])
    for j, x in enumerate(streams):
        for i, last in enumerate(last_rows(x)):
            prev_ref[i, j:j + 1, :] = last
    for i, last in enumerate(last_rows(lora_raw)):
        prevl_ref[i, 0:1, :] = last

    lane = lax.broadcasted_iota(jnp.int32, lora.shape, 1)
    lora = jnp.where(lane < LANES // 2, jnp.tanh(lora), lora)
    proj = jnp.dot(lora.astype(bf16), wl_ref[...], preferred_element_type=f32)
    z = w0_ref[...] + proj[:, :D]
    w_log2 = ((jnp.minimum(z, 0.0) - 0.5) * LOG2_E
              - jnp.log2(1.0 + jnp.exp2(jnp.abs(z) * -LOG2_E)))
    log2_decay = jnp.exp2(w_log2) * -LOG2_E
    a = _sigmoid(a0_ref[...] + proj[:, D:])
    kk = k * kk_ref[...]
    ka = ka_ref[...]
    k = k * (a * ka + (1.0 - ka))

    cum = _chunk_cumsum(ltri_ref[...], log2_decay)
    g_inc = jnp.exp2(cum)
    g_inv = jnp.exp2(-cum)
    g_exc = jnp.exp2(cum - log2_decay)
    g_end = [jnp.exp2(last) for last in last_rows(cum)]

    seg = seg_ref[...]
    row = lax.broadcasted_iota(jnp.int32, (c_len, QUAD), 0)
    col = lax.broadcasted_iota(jnp.int32, (c_len, QUAD), 1)
    col_t = jnp.bitwise_and(col, c_len - 1)
    strict = col_t < row
    incl = col_t <= row
    eye = eye_ref[...]
    head_masks = [hm_ref[g] for g in range(QUAD // RWKV_HEAD)]
    block_diag = (_div_pow2(lax.broadcasted_iota(jnp.int32, (QUAD, QUAD), 0), RWKV_HEAD)
                  == _div_pow2(lax.broadcasted_iota(jnp.int32, (QUAD, QUAD), 1), RWKV_HEAD))

    def stack(y):
        y = y.astype(bf16)
        return jnp.concatenate([y * m for m in head_masks], axis=0)

    def bdot(x, y, dims=NN_DIMS):
        return lax.dot_general(x.astype(bf16), y.astype(bf16), dims, preferred_element_type=f32)

    def qprod(x, y):
        return bdot(x, stack(y))

    slabs = [slice(qd * QUAD, (qd + 1) * QUAD) for qd in range(N_QUADS)]
    items = [(i, qd) for i in halves for qd in range(N_QUADS)]
    each = lambda fn, *cols: [fn(*args) for args in zip(*cols)]
    cut = lambda x: [x[i * c_len:(i + 1) * c_len, slabs[qd]] for i, qd in items]

    kk_q = cut(kk)
    kk_sq = _seg_sums([x * x for x in kk_q], seg)
    kk_q = each(lambda x, sq: x * lax.rsqrt(jnp.maximum(sq, 1e-24)), kk_q, kk_sq)
    a_bar = each(lambda x, g: -x * g, kk_q, cut(g_exc))
    b_til = each(lambda x, a_q, g: x * a_q * g, kk_q, cut(a), cut(g_inv))
    k_til = each(jnp.multiply, cut(k), cut(g_inv))
    r_bar = each(jnp.multiply, cut(r), cut(g_inc))
    g_end_q = [g_end[i][:, slabs[qd]] for i, qd in items]
    b_end = each(jnp.multiply, b_til, g_end_q)
    k_end = each(jnp.multiply, k_til, g_end_q)
    v_q = cut(v)

    pair = each(lambda ab, rb, bt, kt: bdot(jnp.concatenate([ab, rb], axis=0),
                                            jnp.concatenate([stack(bt), stack(kt)], axis=0),
                                            NT_DIMS),
                a_bar, r_bar, b_til, k_til)
    a_ab = [jnp.where(strict, x[:c_len, :QUAD], 0.0) for x in pair]
    a_ak = [jnp.where(strict, x[:c_len, QUAD:], 0.0) for x in pair]
    a_rb = [jnp.where(incl, x[c_len:, :QUAD], 0.0) for x in pair]
    a_rk = [jnp.where(incl, x[c_len:, QUAD:], 0.0) for x in pair]

    t_inv = [eye + x for x in a_ab]
    power = each(qprod, a_ab, a_ab)
    span = 4
    while span < c_len:
        both = each(lambda t, pw: qprod(jnp.concatenate([t, pw], axis=0), pw), t_inv, power)
        t_inv = each(lambda t, x: t + x[:c_len], t_inv, both)
        power = [x[c_len:] for x in both]
        span *= 2
    t_inv = each(lambda t, pw: t + qprod(t, pw), t_inv, power)

    akv = each(lambda aak, ark, vq: qprod(jnp.concatenate([aak, ark], axis=0), vq),
               a_ak, a_rk, v_q)
    wu = each(lambda t, ab, x: bdot(t, jnp.concatenate([stack(ab), stack(x[:c_len])], axis=1)),
              t_inv, a_bar, akv)
    state = [state_ref[i, qd] for i, qd in items]
    on_state = each(lambda x, rb, st: bdot(jnp.concatenate([x[:, :QUAD], rb], axis=0), st,
                                           NT_DIMS), wu, r_bar, state)
    u = each(lambda hs, x: hs[:c_len] + x[:, QUAD:], on_state, wu)
    y = each(lambda hs, arb, u_q, x: hs[c_len:] + qprod(arb, u_q) + x[c_len:],
             on_state, a_rb, u, akv)
    upd = each(lambda u_q, vq, be, ke: bdot(jnp.concatenate([u_q, vq], axis=0),
                                            jnp.concatenate([be, ke], axis=0), TN_DIMS),
               u, v_q, b_end, k_end)
    for n, (i, qd) in enumerate(items):
        state_ref[i, qd] = state[n] * g_end_q[n] + jnp.where(block_diag, upd[n], 0.0)

    inv_n = 1.0 / RWKV_HEAD
    rkr = cut(r * k * rk_ref[...])
    sums = _seg_sums(y + rkr, seg)
    dev = each(lambda y_q, total: y_q - total * inv_n, y, sums[:len(items)])
    bonus = each(jnp.multiply, sums[len(items):], v_q)
    var = [total * inv_n for total in _seg_sums([d_q * d_q for d_q in dev], seg)]
    for n, (i, qd) in enumerate(items):
        sl = slabs[qd]
        y_q = dev[n] * lax.rsqrt(var[n] + RWKV_GN_EPS) * lnw_ref[:, sl] + lnb_ref[:, sl]
        o_ref[i, :, sl] = ((y_q + bonus[n]) * _silu(z_ref[i, :, sl].astype(f32))).astype(bf16)


def _rwkv(p, mu_rkv, mu_lora, w0, a0, w_lora, k_k, k_a, r_k, ln_w, ln_b, seg, ltri, batch, seq):
    nc = seq // CHUNK
    col = jnp.arange(QUAD)
    eye = (col[None, :] % CHUNK == jnp.arange(CHUNK)[:, None]).astype(f32)
    head_masks = jnp.broadcast_to(
        (col[None, None, :] // RWKV_HEAD == jnp.arange(QUAD // RWKV_HEAD)[:, None, None]),
        (QUAD // RWKV_HEAD, CHUNK, QUAD)).astype(bf16)
    lora_col = T_SMALL * (D // LANES) + OFF_LORA // LANES
    tile = lambda t: pl.BlockSpec((PAIR, CHUNK, D), lambda b, c: (b, c, t))
    const = lambda shape: pl.BlockSpec(shape, lambda b, c: (0, 0))
    return pl.pallas_call(
        _rwkv_kernel,
        grid=(batch // PAIR, nc),
        in_specs=[tile(T_R), tile(T_K), tile(T_V),
                  pl.BlockSpec((PAIR, CHUNK, LANES), lambda b, c: (b, c, lora_col)),
                  tile(T_RZ),
                  const(mu_rkv.shape), const(mu_lora.shape), const((1, D)), const((1, D)),
                  const(w_lora.shape),
                  const((1, D)), const((1, D)), const((1, D)), const((1, D)), const((1, D)),
                  const(seg.shape), const(ltri.shape), const(eye.shape),
                  pl.BlockSpec(head_masks.shape, lambda b, c: (0, 0, 0))],
        out_specs=pl.BlockSpec((PAIR, CHUNK, D), lambda b, c: (b, c, 0)),
        out_shape=jax.ShapeDtypeStruct((batch, seq, D), bf16),
        scratch_shapes=[pltpu.VMEM((PAIR, N_QUADS, QUAD, QUAD), f32),
                        pltpu.VMEM((PAIR, 8, D), f32), pltpu.VMEM((PAIR, 8, LANES), f32)],
        compiler_params=pltpu.CompilerParams(
            dimension_semantics=("parallel", "arbitrary"), vmem_limit_bytes=VMEM_LIMIT),
        name="rwkv7_chunk",
    )(p, p, p, p, p, mu_rkv, mu_lora, w0, a0, w_lora, k_k, k_a, r_k, ln_w, ln_b, seg, ltri, eye,
      head_masks)


def _gla_kernel(qk_ref, v_ref, gl_ref, z_ref, w2_ref, gb_ref, ng_ref, ltri_ref, o_ref, state_ref):
    c_len = CHUNK
    kw = GLA_HEADS * GLA_DK
    halves = range(PAIR)

    @pl.when(pl.program_id(1) == 0)
    def _():
        state_ref[...] = jnp.zeros(state_ref.shape, f32)

    qk = qk_ref[...].astype(f32).reshape(PAIR * c_len, 2 * kw)
    q = qk[:, :kw] * (GLA_DK ** -0.5)
    k = qk[:, kw:]
    gl = gl_ref[...].reshape(PAIR * c_len, LANES)
    gate = jnp.dot(gl, w2_ref[...], preferred_element_type=f32) + gb_ref[...]
    log_a = -_softplus(-gate) / GLA_GATE_NORM
    cum = _chunk_cumsum(ltri_ref[...], log_a)
    rows = [slice(i * c_len, (i + 1) * c_len) for i in halves]
    cum_last = [cum[sl.stop - 1:sl.stop, :] for sl in rows]
    q_dec = (q * jnp.exp(cum)).astype(bf16)
    k_inv = (k * jnp.exp(-cum)).astype(bf16)
    k_end = [(k[sl] * jnp.exp(last - cum[sl])).astype(bf16) for sl, last in zip(rows, cum_last)]
    decay = [jnp.exp(last) for last in cum_last]
    causal = (lax.broadcasted_iota(jnp.int32, (c_len, c_len), 1)
              <= lax.broadcasted_iota(jnp.int32, (c_len, c_len), 0))
    ks = [slice(h * GLA_DK, (h + 1) * GLA_DK) for h in range(GLA_HEADS)]
    vs = [slice(h * GLA_DV, (h + 1) * GLA_DV) for h in range(GLA_HEADS)]
    items = [(i, h) for i in halves for h in range(GLA_HEADS)]
    v_h = [v_ref[i, :, vs[h]] for i, h in items]
    state = [state_ref[i, h] for i, h in items]
    attn = [lax.dot_general(q_dec[rows[i], ks[h]], k_inv[rows[i], ks[h]], NT_DIMS,
                            preferred_element_type=f32) for i, h in items]
    attn = [jnp.where(causal, x, 0.0).astype(bf16) for x in attn]
    o = [jnp.dot(attn[n], v_h[n], preferred_element_type=f32)
         + lax.dot_general(q_dec[rows[i], ks[h]], state[n].astype(bf16), NT_DIMS,
                           preferred_element_type=f32) for n, (i, h) in enumerate(items)]
    for n, (i, h) in enumerate(items):
        state_ref[i, h] = state[n] * decay[i][:, ks[h]] + lax.dot_general(
            v_h[n], k_end[i][:, ks[h]], TN_DIMS, preferred_element_type=f32)
    for n, (i, h) in enumerate(items):
        o_h = o[n] * lax.rsqrt(jnp.mean(o[n] * o[n], axis=-1, keepdims=True) + NORM_EPS)
        gate_z = _silu(z_ref[i, :, vs[h]].astype(f32))
        o_ref[i, :, vs[h]] = (o_h * ng_ref[...] * gate_z).astype(bf16)


def _gla(p, w2_pad, g_bias, norm_g, ltri, batch, seq):
    nc = seq // CHUNK
    slab_col = T_SMALL * (D // LANES) + OFF_SLAB // LANES
    tile = lambda t: pl.BlockSpec((PAIR, CHUNK, D), lambda b, c: (b, c, t))
    const = lambda shape: pl.BlockSpec(shape, lambda b, c: (0, 0))
    return pl.pallas_call(
        _gla_kernel,
        grid=(batch // PAIR, nc),
        in_specs=[tile(T_GQK), tile(T_GV),
                  pl.BlockSpec((PAIR, CHUNK, LANES), lambda b, c: (b, c, slab_col)),
                  tile(T_GZ),
                  const(w2_pad.shape), const(g_bias.shape), const(norm_g.shape),
                  const(ltri.shape)],
        out_specs=pl.BlockSpec((PAIR, CHUNK, D), lambda b, c: (b, c, 0)),
        out_shape=jax.ShapeDtypeStruct((batch, seq, D), bf16),
        scratch_shapes=[pltpu.VMEM((PAIR, GLA_HEADS, GLA_DV, GLA_DK), f32)],
        compiler_params=pltpu.CompilerParams(
            dimension_semantics=("parallel", "arbitrary"), vmem_limit_bytes=VMEM_LIMIT),
        name="gla_chunk",
    )(p, p, p, p, w2_pad, g_bias, norm_g, ltri)


def _merge_kernel(om_ref, or_ref, og_ref, ga_ref, gb_ref, gc_ref, x_ref, gate_ref, np_ref,
                  wm_ref, wr_ref, wg_ref, wo_ref, o_ref):
    gate_of = lambda ref: _sigmoid(ref[...].astype(f32))
    merged = (gate_of(ga_ref) * jnp.dot(om_ref[...], wm_ref[...], preferred_element_type=f32)
              + gate_of(gb_ref) * jnp.dot(or_ref[...], wr_ref[...], preferred_element_type=f32)
              + gate_of(gc_ref) * jnp.dot(og_ref[...], wg_ref[...], preferred_element_type=f32))
    y = jnp.dot(merged.astype(bf16), wo_ref[...], preferred_element_type=f32)
    o_ref[...] = x_ref[...] + gate_ref[0] * _rms(y, np_ref[...])


def _merge(o_mla, o_rwkv, o_gla, p, x2d, gate, norm_post, w_mla, w_rwkv, w_gla, w_out, seq):
    tokens = x2d.shape[0]
    tm = min(512, seq)
    per_b = seq // tm
    row = lambda i: (i, 0)
    const = lambda i: (0, 0)
    return pl.pallas_call(
        _merge_kernel,
        grid=(tokens // tm,),
        in_specs=[pl.BlockSpec((tm, D), row), pl.BlockSpec((tm, D), row),
                  pl.BlockSpec((tm, D), row),
                  pl.BlockSpec((tm, D), lambda i: (i, T_GA)),
                  pl.BlockSpec((tm, D), lambda i: (i, T_GB)),
                  pl.BlockSpec((tm, D), lambda i: (i, T_GC)),
                  pl.BlockSpec((tm, D), row),
                  pl.BlockSpec((1, 1, D), lambda i: (i // per_b, 0, 0)),
                  pl.BlockSpec((1, D), const),
                  pl.BlockSpec((D, D), const), pl.BlockSpec((D, D), const),
                  pl.BlockSpec((D, D), const), pl.BlockSpec((D, D), const)],
        out_specs=pl.BlockSpec((tm, D), row),
        out_shape=jax.ShapeDtypeStruct((tokens, D), f32),
        compiler_params=pltpu.CompilerParams(
            dimension_semantics=("parallel",), vmem_limit_bytes=VMEM_LIMIT),
        name="merge_outproj",
    )(o_mla, o_rwkv, o_gla, p, p, p, x2d, gate, norm_post, w_mla, w_rwkv, w_gla, w_out)


def _rot_half_cols(w):
    half = w.shape[-1] // 2
    return jnp.concatenate([-w[..., half:], w[..., :half]], axis=-1)


def _permute_w_in(w):
    sizes = (512, 256, 64, 1024, 1024, 1024, 1024, 64, 64, 1024, 512, 512, 1024, 16, 1024,
             1024, 1024, 1024)
    parts = []
    start = 0
    for size in sizes:
        parts.append(w[:, start:start + size])
        start += size
    (c_q, c_kv, k_rope, mla_z, r, k, v, w_l, a_l, r_z, g_q, g_k, g_v, g_l, g_z,
     gate_a, gate_b, gate_c) = parts
    pad = jnp.zeros((w.shape[0], OFF_LORA - OFF_SLAB - MLA_ROPE - g_l.shape[1]), w.dtype)
    return jnp.concatenate([c_q, c_kv, k_rope, g_l, pad, w_l, a_l, mla_z, r, k, v, r_z,
                            g_q, g_k, g_v, g_z, gate_a, gate_b, gate_c], axis=1).astype(bf16)


def _mla_weights(w_uq, w_ukv):
    n_in = w_uq.shape[0]
    wq = w_uq.reshape(n_in, MLA_HEADS, MLA_NOPE + MLA_ROPE)
    wq = jnp.concatenate([wq, _rot_half_cols(wq[..., MLA_NOPE:])], axis=-1)
    wkv = w_ukv.reshape(w_ukv.shape[0], MLA_HEADS, MLA_NOPE + MLA_V)
    return wq.transpose(1, 0, 2).astype(bf16), wkv.transpose(1, 0, 2).astype(bf16)


def kernel(x, c, positions, ada_w, ada_b, norm_pre, norm_post, w_in, rwkv_mu, mla_q_norm,
           mla_kv_norm, mla_w_uq, mla_w_ukv, mla_w_o, rwkv_w0, rwkv_w2, rwkv_a0, rwkv_a2,
           rwkv_k_k, rwkv_k_a, rwkv_r_k, rwkv_ln_w, rwkv_ln_b, rwkv_w_o, gla_w2, gla_b,
           gla_norm, gla_w_o, w_out):
    batch, seq, _ = x.shape
    assert batch % PAIR == 0 and seq % (2 * CHUNK) == 0, (batch, seq)
    depth = w_in.shape[0]
    tokens = batch * seq
    x2d = x.reshape(tokens, D)
    rope = _rope_table(positions, tokens)
    idx = jnp.arange(SEG_WIDTH)
    seg = (idx[:, None] // RWKV_HEAD == idx[None, :] // RWKV_HEAD).astype(bf16)
    tri = jnp.arange(min(PAIR * CHUNK, SEG_WIDTH))
    ltri = ((tri[None, :] <= tri[:, None])
            & (tri[None, :] // CHUNK == tri[:, None] // CHUNK)).astype(bf16)
    row1 = lambda t: t.reshape(1, -1)

    mod = _modulation(c, ada_w, ada_b)
    for l in range(depth):
        shift = mod[l, :, :D].reshape(batch, 1, D)
        scale = mod[l, :, D:2 * D].reshape(batch, 1, D)
        gate = mod[l, :, 2 * D:].reshape(batch, 1, D)
        p = _inproj(x2d, scale, shift, row1(norm_pre[l]), _permute_w_in(w_in[l]), seq)

        wq, wkv = _mla_weights(mla_w_uq[l], mla_w_ukv[l])
        q, k, v = _mla_prep(p, rope, row1(mla_q_norm[l]), row1(mla_kv_norm[l]), wq, wkv,
                            batch, seq)
        o_mla = _flash(q, k, v, p, batch, seq).reshape(tokens, D)

        mu = rwkv_mu[l]
        mu_rkv = jnp.zeros((8, D), f32).at[:3].set(mu[:3 * D].reshape(3, D))
        w_lora = jnp.zeros((LANES, 2 * D), f32)
        w_lora = w_lora.at[:LANES // 2, :D].set(rwkv_w2[l]).at[LANES // 2:, D:].set(rwkv_a2[l])
        p3 = p.reshape(batch, seq, P_COLS)
        o_rwkv = _rwkv(p3, mu_rkv, row1(mu[3 * D:]), row1(rwkv_w0[l]), row1(rwkv_a0[l]),
                       w_lora.astype(bf16), row1(rwkv_k_k[l]), row1(rwkv_k_a[l]),
                       row1(rwkv_r_k[l]), row1(rwkv_ln_w[l]), row1(rwkv_ln_b[l]), seg, ltri,
                       batch, seq).reshape(tokens, D)

        w2_pad = jnp.zeros((LANES, GLA_HEADS * GLA_DK), f32)
        w2_pad = w2_pad.at[OFF_GL:OFF_GL + gla_w2.shape[1]].set(gla_w2[l]).astype(bf16)
        o_gla = _gla(p3, w2_pad, row1(gla_b[l]), row1(gla_norm[l]), ltri, batch,
                     seq).reshape(tokens, D)

        x2d = _merge(o_mla, o_rwkv, o_gla, p, x2d, gate, row1(norm_post[l]),
                     mla_w_o[l].astype(bf16), rwkv_w_o[l].astype(bf16),
                     gla_w_o[l].astype(bf16), w_out[l].astype(bf16), seq)
    return x2d.reshape(batch, seq, D)
```
